```python
import jax
import jax.numpy as jnp
from jax import lax
import numpy as np

D_MODEL = 2048
BATCH = 4
SEQ = 4096
DEPTH = 1
DEC_BATCH = 128
DEC_SEQ = 4
PAST_LEN = 16384
PAGE_SIZE = 128

N_Q_HEADS = 32
N_KV_HEADS = 4
HEAD_DIM = 64
Q_PER_KV = N_Q_HEADS // N_KV_HEADS
WINDOW = 128
ATTN_BLOCK = 128
POOL_WIDTH = D_MODEL // 2
POOL_WINDOWS = (2, 4, 8, 16)
N_POOL_GROUPS = len(POOL_WINDOWS)
POOL_GROUP_WIDTH = POOL_WIDTH // N_POOL_GROUPS
POOL_STATE = max(POOL_WINDOWS) - 1
Q_WIDTH = N_Q_HEADS * HEAD_DIM
KV_WIDTH = N_KV_HEADS * HEAD_DIM
IN_WIDTH = Q_WIDTH + 2 * KV_WIDTH + POOL_WIDTH + 2 * D_MODEL
N_EXPERTS = 32
TOP_K = 4
D_FF = D_MODEL
SWIGLU_LIMIT = 7.0
SWIGLU_ALPHA = 1.702
MOE_BLOCK = 128
RMS_EPS = 1e-5

kernel_name = 'hybrid_pool_swa_moe_decode_step'


def rms_norm(x, gain):
    xf = x.astype(jnp.float32)
    y = xf * lax.rsqrt(jnp.mean(xf * xf, axis=-1, keepdims=True) + RMS_EPS)
    return (y * gain.astype(jnp.float32)).astype(x.dtype)


def split_projection(z):
    offsets = np.cumsum([Q_WIDTH, KV_WIDTH, KV_WIDTH, POOL_WIDTH, D_MODEL]).tolist()
    return jnp.split(z, offsets, axis=-1)


def alibi_slopes():
    h = jnp.arange(1, N_Q_HEADS + 1, dtype=jnp.float32)
    return jnp.exp2(-8.0 * h / N_Q_HEADS).reshape(N_KV_HEADS, Q_PER_KV)


def sink_attention(q, k, v, dist, valid, sinks):
    s = jnp.einsum('...qgrd,...kgd->...grqk', q, k, preferred_element_type=jnp.float32)
    s = s * (HEAD_DIM ** -0.5) - alibi_slopes()[:, :, None, None] * dist.astype(jnp.float32)
    s = jnp.where(valid, s, -jnp.inf)
    sink = jnp.broadcast_to(sinks.astype(jnp.float32).reshape(N_KV_HEADS, Q_PER_KV, 1, 1),
                            s.shape[:-1] + (1,))
    p = jax.nn.softmax(jnp.concatenate([s, sink], axis=-1), axis=-1)[..., :-1]
    return jnp.einsum('...grqk,...kgd->...qgrd', p.astype(v.dtype), v)


def prompt_window_attention(q, k, v, sinks):
    b, s = q.shape[:2]
    nb = s // ATTN_BLOCK
    qb = q.reshape(b, nb, ATTN_BLOCK, N_KV_HEADS, Q_PER_KV, HEAD_DIM)
    kb = k.reshape(b, nb, ATTN_BLOCK, N_KV_HEADS, HEAD_DIM)
    vb = v.reshape(b, nb, ATTN_BLOCK, N_KV_HEADS, HEAD_DIM)
    prev = lambda a: jnp.concatenate([jnp.zeros_like(a[:, :1]), a[:, :-1]], axis=1)
    kk = jnp.concatenate([prev(kb), kb], axis=2)
    vv = jnp.concatenate([prev(vb), vb], axis=2)
    qi = jnp.arange(ATTN_BLOCK)[:, None]
    kj = jnp.arange(2 * ATTN_BLOCK)[None, :]
    dist = qi + ATTN_BLOCK - kj
    band = (dist >= 0) & (dist <= WINDOW)
    exists = (jnp.arange(nb)[:, None, None] > 0) | (kj[None] >= ATTN_BLOCK)
    valid = (band[None] & exists)[:, None, None]
    o = sink_attention(qb, kk, vv, dist, valid, sinks)
    return o.reshape(b, s, Q_WIDTH)


def sample_window_attention(q, k, v, cache_k, cache_v, sinks):
    bd, n = q.shape[:2]
    n_rows = cache_k.shape[1]
    kk = jnp.concatenate([cache_k, k], axis=1)
    vv = jnp.concatenate([cache_v, v], axis=1)
    dist = (n_rows + jnp.arange(n))[:, None] - jnp.arange(n_rows + n)[None, :]
    valid = (dist >= 0) & (dist <= WINDOW)
    o = sink_attention(q, kk, vv, dist, valid, sinks)
    return o.reshape(bd, n, Q_WIDTH), kk[:, -n_rows:], vv[:, -n_rows:]


def multiscale_pool(u, prefix, pos0):
    n_prev = prefix.shape[1]
    n_new = u.shape[1]
    ext = jnp.concatenate([prefix, u], axis=1).astype(jnp.float32)
    csum = jnp.concatenate([jnp.zeros_like(ext[:, :1]), jnp.cumsum(ext, axis=1)], axis=1)
    pos = pos0 + jnp.arange(n_new)
    means = []
    for g, w in enumerate(POOL_WINDOWS):
        ch = slice(g * POOL_GROUP_WIDTH, (g + 1) * POOL_GROUP_WIDTH)
        wsum = csum[:, n_prev + 1:, ch] - csum[:, n_prev + 1 - w:n_prev + 1 - w + n_new, ch]
        cnt = jnp.minimum(w, pos + 1).astype(jnp.float32)
        means.append(wsum / cnt[None, :, None])
    return (jnp.concatenate(means, axis=-1) - u.astype(jnp.float32)).astype(u.dtype)


def pool_branch(pooled, w_mix, scale):
    lead = pooled.shape[:-1]
    z = jnp.einsum('...gc,gcd->...gd', pooled.reshape(lead + (N_POOL_GROUPS, POOL_GROUP_WIDTH)), w_mix)
    return z.reshape(lead + (POOL_WIDTH,)) * scale


def merge_branches(attn_o, pool_o, gate_attn, gate_pool, w_attn_out, w_pool_out, w_out):
    a = attn_o @ w_attn_out
    p = pool_o @ w_pool_out
    return (jax.nn.sigmoid(gate_attn) * a + jax.nn.sigmoid(gate_pool) * p) @ w_out


def routed_experts(x, w_router, b_router, w_gate, b_gate, w_up, b_up, w_down, b_down):
    t, d = x.shape
    logits = (x @ w_router).astype(jnp.float32) + b_router.astype(jnp.float32)
    top_val, top_idx = lax.top_k(logits, TOP_K)
    gates = jax.nn.softmax(top_val, axis=-1)
    tk = t * TOP_K
    flat_e = top_idx.reshape(-1)
    flat_tok = jnp.repeat(jnp.arange(t, dtype=jnp.int32), TOP_K)
    flat_g = gates.reshape(-1)
    order = jnp.argsort(flat_e)
    se = flat_e[order]
    counts = jnp.bincount(flat_e, length=N_EXPERTS)
    padded = (counts + MOE_BLOCK - 1) // MOE_BLOCK * MOE_BLOCK
    pad_end = jnp.cumsum(padded)
    pad_start = pad_end - padded
    start = jnp.cumsum(counts) - counts
    dest = pad_start[se] + jnp.arange(tk) - start[se]
    n_blocks = -(-tk // MOE_BLOCK) + N_EXPERTS
    n_slots = n_blocks * MOE_BLOCK
    slot_tok = jnp.full((n_slots,), t, jnp.int32).at[dest].set(flat_tok[order])
    slot_gate = jnp.zeros((n_slots,), jnp.float32).at[dest].set(flat_g[order])
    block_expert = jnp.minimum(
        jnp.searchsorted(pad_end, jnp.arange(n_blocks) * MOE_BLOCK, side='right'), N_EXPERTS - 1)
    x_pad = jnp.concatenate([x, jnp.zeros((1, d), x.dtype)], axis=0)

    def expert_block(args):
        e, tok = args
        xb = x_pad[tok]
        glu = jnp.minimum(xb @ w_gate[e] + b_gate[e], SWIGLU_LIMIT)
        lin = jnp.clip(xb @ w_up[e] + b_up[e], -SWIGLU_LIMIT, SWIGLU_LIMIT)
        h = glu * jax.nn.sigmoid(SWIGLU_ALPHA * glu) * (lin + 1)
        return h @ w_down[e] + b_down[e]

    y = lax.map(expert_block, (block_expert, slot_tok.reshape(n_blocks, MOE_BLOCK)))
    y = y.reshape(n_slots, d) * slot_gate[:, None].astype(y.dtype)
    return jnp.zeros((t + 1, d), y.dtype).at[slot_tok].add(y)[:t]


def setup_inputs(seed: int = 0) -> dict:
    key = jax.random.key(seed)
    ks = jax.random.split(key, 24)
    nrm = lambda k, shape, scale: jax.random.normal(k, shape, jnp.float32) * scale
    win_rows = min(WINDOW, PAST_LEN)
    return {
        'x_prompt': nrm(ks[0], (BATCH, SEQ, D_MODEL), 1.0),
        'x_sample': nrm(ks[1], (DEC_BATCH, DEC_SEQ, D_MODEL), 1.0),
        'cache_k': nrm(ks[2], (DEPTH, DEC_BATCH, win_rows, N_KV_HEADS, HEAD_DIM), 1.0),
        'cache_v': nrm(ks[3], (DEPTH, DEC_BATCH, win_rows, N_KV_HEADS, HEAD_DIM), 1.0),
        'state_pool': nrm(ks[4], (DEPTH, DEC_BATCH, POOL_STATE, POOL_WIDTH), 1.0),
        'norm_mix': 1.0 + nrm(ks[5], (DEPTH, D_MODEL), 0.05),
        'w_in': nrm(ks[6], (DEPTH, D_MODEL, IN_WIDTH), D_MODEL ** -0.5),
        'attn_sinks': nrm(ks[7], (DEPTH, N_Q_HEADS), 1.0),
        'w_pool_mix': nrm(ks[8], (DEPTH, N_POOL_GROUPS, POOL_GROUP_WIDTH, POOL_GROUP_WIDTH), POOL_GROUP_WIDTH ** -0.5),
        'pool_scale': 1.0 + nrm(ks[9], (DEPTH, POOL_WIDTH), 0.1),
        'w_attn_out': nrm(ks[10], (DEPTH, Q_WIDTH, D_MODEL), Q_WIDTH ** -0.5),
        'w_pool_out': nrm(ks[11], (DEPTH, POOL_WIDTH, D_MODEL), POOL_WIDTH ** -0.5),
        'w_out': nrm(ks[12], (DEPTH, D_MODEL, D_MODEL), D_MODEL ** -0.5),
        'norm_ffn': 1.0 + nrm(ks[13], (DEPTH, D_MODEL), 0.05),
        'w_router': nrm(ks[14], (DEPTH, D_MODEL, N_EXPERTS), D_MODEL ** -0.5),
        'b_router': nrm(ks[15], (DEPTH, N_EXPERTS), 0.01),
        'w_gate': nrm(ks[16], (DEPTH, N_EXPERTS, D_MODEL, D_FF), D_MODEL ** -0.5),
        'b_gate': nrm(ks[17], (DEPTH, N_EXPERTS, D_FF), 0.02),
        'w_up': nrm(ks[18], (DEPTH, N_EXPERTS, D_MODEL, D_FF), D_MODEL ** -0.5),
        'b_up': nrm(ks[19], (DEPTH, N_EXPERTS, D_FF), 0.02),
        'w_down': nrm(ks[20], (DEPTH, N_EXPERTS, D_FF, D_MODEL), D_FF ** -0.5),
        'b_down': nrm(ks[21], (DEPTH, N_EXPERTS, D_MODEL), 0.02),
        'norm_final': 1.0 + nrm(ks[22], (D_MODEL,), 0.05),
    }


def reference(x_prompt, x_sample, cache_k, cache_v, state_pool,
              norm_mix, w_in, attn_sinks, w_pool_mix, pool_scale,
              w_attn_out, w_pool_out, w_out, norm_ffn, w_router, b_router,
              w_gate, b_gate, w_up, b_up, w_down, b_down, norm_final):
    bp, sp, d = x_prompt.shape
    bs, ns, _ = x_sample.shape
    keep_p = min(WINDOW, sp)
    hp, hs = x_prompt, x_sample
    k_p, v_p, pool_p, k_s, v_s, pool_s = [], [], [], [], [], []
    for l in range(DEPTH):
        q, k, v, u, ga, gp = split_projection(rms_norm(hp, norm_mix[l]) @ w_in[l])
        k = k.reshape(bp, sp, N_KV_HEADS, HEAD_DIM)
        v = v.reshape(bp, sp, N_KV_HEADS, HEAD_DIM)
        attn = prompt_window_attention(q.reshape(bp, sp, N_KV_HEADS, Q_PER_KV, HEAD_DIM), k, v, attn_sinks[l])
        pooled = multiscale_pool(u, jnp.zeros((bp, POOL_STATE, POOL_WIDTH), u.dtype), 0)
        hp = hp + merge_branches(attn, pool_branch(pooled, w_pool_mix[l], pool_scale[l]), ga, gp,
                                 w_attn_out[l], w_pool_out[l], w_out[l])
        k_p.append(k[:, sp - keep_p:])
        v_p.append(v[:, sp - keep_p:])
        pool_p.append(u[:, sp - POOL_STATE:])
        q, k, v, u, ga, gp = split_projection(rms_norm(hs, norm_mix[l]) @ w_in[l])
        attn, nk, nv = sample_window_attention(
            q.reshape(bs, ns, N_KV_HEADS, Q_PER_KV, HEAD_DIM),
            k.reshape(bs, ns, N_KV_HEADS, HEAD_DIM), v.reshape(bs, ns, N_KV_HEADS, HEAD_DIM),
            cache_k[l], cache_v[l], attn_sinks[l])
        pooled = multiscale_pool(u, state_pool[l], PAST_LEN)
        hs = hs + merge_branches(attn, pool_branch(pooled, w_pool_mix[l], pool_scale[l]), ga, gp,
                                 w_attn_out[l], w_pool_out[l], w_out[l])
        k_s.append(nk)
        v_s.append(nv)
        pool_s.append(jnp.concatenate([state_pool[l], u], axis=1)[:, -POOL_STATE:])
        tokens = jnp.concatenate([rms_norm(hp, norm_ffn[l]).reshape(-1, d),
                                  rms_norm(hs, norm_ffn[l]).reshape(-1, d)], axis=0)
        f = routed_experts(tokens, w_router[l], b_router[l], w_gate[l], b_gate[l],
                           w_up[l], b_up[l], w_down[l], b_down[l])
        hp = hp + f[:bp * sp].reshape(bp, sp, d)
        hs = hs + f[bp * sp:].reshape(bs, ns, d)
    y_prompt = rms_norm(hp, norm_final)
    y_sample = rms_norm(hs, norm_final)
    return (y_prompt, y_sample, jnp.stack(k_p), jnp.stack(v_p), jnp.stack(pool_p),
            jnp.stack(k_s), jnp.stack(v_s), jnp.stack(pool_s))
```

```python
import functools

import jax
import jax.numpy as jnp
import numpy as np
from jax import lax
from jax.experimental import pallas as pl
from jax.experimental.pallas import tpu as pltpu

F32 = jnp.float32
BF16 = jnp.bfloat16
I32 = jnp.int32

WINDOW = 128
POOL_WINDOWS = (2, 4, 8, 16)
POOL_STATE = max(POOL_WINDOWS) - 1
PAST_LEN = 16384
TOP_K = 4
SWIGLU_LIMIT = 7.0
SWIGLU_ALPHA = 1.702
RMS_EPS = 1e-5
NEG_BIG = -1e30

V7X_VMEM_BYTES = 64 * 1024 * 1024
EXPERT_ROWS = 256
HALO = 16


def _vmem_limit(nbytes):
    return int(min(nbytes + (8 << 20), V7X_VMEM_BYTES - (4 << 20)))


def _rms(x, gain):
    ms = jnp.mean(x * x, axis=-1, keepdims=True)
    return x * lax.rsqrt(ms + RMS_EPS) * gain


def _norm_matmul_kernel(x_ref, g_ref, w_ref, o_ref, xn_ref, *, epilogue):
    @pl.when(pl.program_id(1) == 0)
    def _():
        xn_ref[...] = _rms(x_ref[...], g_ref[...]).astype(BF16)

    acc = jnp.dot(xn_ref[...], w_ref[...], preferred_element_type=F32)
    if epilogue == "sigmoid":
        acc = jax.nn.sigmoid(acc)
    elif epilogue == "scale":
        acc = acc * 0.125
    o_ref[...] = acc.astype(o_ref.dtype)


def _norm_matmul(x, gain, w, out_dtype, epilogue, tm, tn, name):
    t, d = x.shape
    n = w.shape[1]
    tm = min(tm, t)
    tn = min(tn, n)
    assert t % tm == 0 and n % tn == 0
    est = 2 * tm * d * 4 + tm * d * 2 + 2 * d * tn * 2 + 2 * tm * tn * 4
    return pl.pallas_call(
        functools.partial(_norm_matmul_kernel, epilogue=epilogue),
        out_shape=jax.ShapeDtypeStruct((t, n), out_dtype),
        grid=(t // tm, n // tn),
        in_specs=[
            pl.BlockSpec((tm, d), lambda i, j: (i, 0)),
            pl.BlockSpec((1, d), lambda i, j: (0, 0)),
            pl.BlockSpec((d, tn), lambda i, j: (0, j)),
        ],
        out_specs=pl.BlockSpec((tm, tn), lambda i, j: (i, j)),
        scratch_shapes=[pltpu.VMEM((tm, d), BF16)],
        compiler_params=pltpu.CompilerParams(
            dimension_semantics=("parallel", "arbitrary"),
            vmem_limit_bytes=_vmem_limit(est)),
        name=name,
    )(x, gain, w)


def _attn_prompt_kernel(sink_ref, q_ref, kc_ref, vc_ref, kp_ref, vp_ref, o_ref, *,
                        n_kv, q_per_kv, hd, blocks_per_seq, slopes):
    blk = WINDOW
    has_prev = (pl.program_id(0) % blocks_per_seq) > 0
    qi = lax.broadcasted_iota(I32, (blk, 2 * blk), 0)
    kj = lax.broadcasted_iota(I32, (blk, 2 * blk), 1)
    dist = qi + blk - kj
    valid = (dist >= 0) & (dist <= WINDOW) & ((kj >= blk) | has_prev)
    distf = dist.astype(F32)
    for g in range(n_kv):
        cs = slice(g * hd, (g + 1) * hd)
        k = jnp.concatenate([kp_ref[:, cs], kc_ref[:, cs]], axis=0).astype(BF16)
        v = jnp.concatenate([vp_ref[:, cs], vc_ref[:, cs]], axis=0).astype(BF16)
        for r in range(q_per_kv):
            h = g * q_per_kv + r
            hs = slice(h * hd, (h + 1) * hd)
            s = lax.dot_general(q_ref[:, hs], k, (((1,), (1,)), ((), ())),
                                preferred_element_type=F32)
            s = jnp.where(valid, s - slopes[h] * distf, NEG_BIG)
            sink = sink_ref[h]
            m = jnp.maximum(jnp.max(s, axis=-1, keepdims=True), sink)
            p = jnp.exp(s - m)
            denom = jnp.sum(p, axis=-1, keepdims=True) + jnp.exp(sink - m)
            o = jnp.dot(p.astype(BF16), v, preferred_element_type=F32)
            o_ref[:, hs] = (o / denom).astype(o_ref.dtype)


def _attn_prompt(q, kvu, sinks, seq, n_kv, hd):
    t, qw = q.shape
    n_q = qw // hd
    kvw = n_kv * hd
    blk = WINDOW
    assert t % blk == 0 and seq % blk == 0
    slopes = tuple(float(2.0 ** (-8.0 * (h + 1) / n_q)) for h in range(n_q))
    prev = lambda i: jnp.maximum(i - 1, 0)
    return pl.pallas_call(
        functools.partial(_attn_prompt_kernel, n_kv=n_kv, q_per_kv=n_q // n_kv, hd=hd,
                          blocks_per_seq=seq // blk, slopes=slopes),
        out_shape=jax.ShapeDtypeStruct((t, qw), BF16),
        grid=(t // blk,),
        in_specs=[
            pl.BlockSpec(memory_space=pltpu.SMEM),
            pl.BlockSpec((blk, qw), lambda i: (i, 0)),
            pl.BlockSpec((blk, kvw), lambda i: (i, 0)),
            pl.BlockSpec((blk, kvw), lambda i: (i, 1)),
            pl.BlockSpec((blk, kvw), lambda i: (prev(i), 0)),
            pl.BlockSpec((blk, kvw), lambda i: (prev(i), 1)),
        ],
        out_specs=pl.BlockSpec((blk, qw), lambda i: (i, 0)),
        compiler_params=pltpu.CompilerParams(dimension_semantics=("parallel",)),
        name="attn_prompt",
    )(sinks, q, kvu, kvu, kvu, kvu)


def _attn_sample_kernel(qbd_ref, ck_ref, cv_ref, kn_ref, vn_ref, bc_ref, bn_ref, sink_ref, o_ref, *,
                        bb, n_kv, hd):
    nt = (((1,), (1,)), ((), ()))
    rows = qbd_ref.shape[1]
    grp = lax.broadcasted_iota(I32, (rows, 1), 0) // (rows // n_kv)
    sink = sink_ref[...]
    for b in range(bb):
        qb = qbd_ref[b]
        s_c = lax.dot_general(qb, ck_ref[b].astype(BF16), nt, preferred_element_type=F32) + bc_ref[...]
        s_n = lax.dot_general(qb, kn_ref[b].astype(BF16), nt, preferred_element_type=F32) + bn_ref[...]
        m = jnp.maximum(jnp.maximum(jnp.max(s_c, axis=-1, keepdims=True),
                                    jnp.max(s_n, axis=-1, keepdims=True)), sink)
        p_c = jnp.exp(s_c - m)
        p_n = jnp.exp(s_n - m)
        denom = (jnp.sum(p_c, axis=-1, keepdims=True) + jnp.sum(p_n, axis=-1, keepdims=True)
                 + jnp.exp(sink - m))
        o = (jnp.dot(p_c.astype(BF16), cv_ref[b].astype(BF16), preferred_element_type=F32)
             + jnp.dot(p_n.astype(BF16), vn_ref[b].astype(BF16), preferred_element_type=F32))
        sel = jnp.zeros((rows, hd), F32)
        for g in range(n_kv):
            sel = sel + jnp.where(grp == g, o[:, g * hd:(g + 1) * hd], 0.0)
        o_ref[b] = sel / denom


def _attn_sample(q_s, k_new, v_new, cache_k, cache_v, sinks, n_kv, hd):
    bd, n, qw = q_s.shape
    n_q = qw // hd
    r = n_q // n_kv
    kvw = n_kv * hd
    w = cache_k.shape[1]
    rows = n_q * n
    npad = 16
    q5 = q_s.reshape(bd, n, n_kv, r, hd)
    qbd = jnp.einsum("bigrd,gh->bgrihd", q5, jnp.eye(n_kv, dtype=q_s.dtype)).reshape(bd, rows, kvw)
    pad = ((0, 0), (0, npad - n), (0, 0))
    k_new = jnp.pad(k_new, pad)
    v_new = jnp.pad(v_new, pad)
    slopes = 2.0 ** (-8.0 * np.arange(1, n_q + 1, dtype=np.float64) / n_q)
    slope_c = np.repeat(slopes, n)
    i_c = np.tile(np.arange(n), n_q)
    dist_c = (w + i_c)[:, None] - np.arange(w)[None, :]
    bias_c = np.where((dist_c >= 0) & (dist_c <= WINDOW), -slope_c[:, None] * dist_c, NEG_BIG)
    dist_n = i_c[:, None] - np.arange(npad)[None, :]
    ok_n = (dist_n >= 0) & (np.arange(npad)[None, :] < n)
    bias_n = np.where(ok_n, -slope_c[:, None] * dist_n, NEG_BIG)
    sink_c = jnp.repeat(sinks.astype(F32), n).reshape(rows, 1)
    bb = 8 if bd % 8 == 0 else 1
    return pl.pallas_call(
        functools.partial(_attn_sample_kernel, bb=bb, n_kv=n_kv, hd=hd),
        out_shape=jax.ShapeDtypeStruct((bd, rows, hd), F32),
        grid=(bd // bb,),
        in_specs=[
            pl.BlockSpec((bb, rows, kvw), lambda i: (i, 0, 0)),
            pl.BlockSpec((bb, w, kvw), lambda i: (i, 0, 0)),
            pl.BlockSpec((bb, w, kvw), lambda i: (i, 0, 0)),
            pl.BlockSpec((bb, npad, kvw), lambda i: (i, 0, 0)),
            pl.BlockSpec((bb, npad, kvw), lambda i: (i, 0, 0)),
            pl.BlockSpec((rows, w), lambda i: (0, 0)),
            pl.BlockSpec((rows, npad), lambda i: (0, 0)),
            pl.BlockSpec((rows, 1), lambda i: (0, 0)),
        ],
        out_specs=pl.BlockSpec((bb, rows, hd), lambda i: (i, 0, 0)),
        compiler_params=pltpu.CompilerParams(dimension_semantics=("parallel",)),
        name="attn_sample",
    )(qbd, cache_k, cache_v, k_new, v_new,
      jnp.asarray(bias_c, F32), jnp.asarray(bias_n, F32), sink_c)


def _pool_prompt_kernel(*refs, tp, gw, tiles_per_seq):
    ng = len(POOL_WINDOWS)
    cur = refs[:ng]
    halo = refs[ng:2 * ng]
    wmix_ref, scale_ref, o_ref, ext_ref = refs[2 * ng:]
    tile_in_seq = pl.program_id(0) % tiles_per_seq
    first = tile_in_seq == 0
    pos = lax.broadcasted_iota(I32, (tp, 1), 0) + tile_in_seq * tp
    for g, w in enumerate(POOL_WINDOWS):
        u = cur[g][...]
        ext_ref[0:HALO, :] = jnp.where(first, 0.0, halo[g][...])
        ext_ref[HALO:HALO + tp, :] = u
        acc = u
        for d in range(1, w):
            acc = acc + ext_ref[HALO - d:HALO - d + tp, :]
        cnt = jnp.minimum(w, pos + 1).astype(F32)
        pooled = acc / cnt - u
        z = jnp.dot(pooled.astype(BF16), wmix_ref[g], preferred_element_type=F32)
        o_ref[:, g * gw:(g + 1) * gw] = (z * scale_ref[:, g * gw:(g + 1) * gw]).astype(o_ref.dtype)


def _pool_prompt(kvu, u_col0, pw, wmix, scale, seq, tp):
    t = kvu.shape[0]
    ng = len(POOL_WINDOWS)
    gw = pw // ng
    tp = min(tp, seq)
    assert seq % tp == 0 and tp % HALO == 0 and u_col0 % gw == 0
    c0 = u_col0 // gw
    hb = tp // HALO
    cur_specs = [pl.BlockSpec((tp, gw), functools.partial(lambda i, g: (i, c0 + g), g=g)) for g in range(ng)]
    halo_specs = [pl.BlockSpec((HALO, gw),
                               functools.partial(lambda i, g: (jnp.maximum(i * hb - 1, 0), c0 + g), g=g))
                  for g in range(ng)]
    return pl.pallas_call(
        functools.partial(_pool_prompt_kernel, tp=tp, gw=gw, tiles_per_seq=seq // tp),
        out_shape=jax.ShapeDtypeStruct((t, pw), BF16),
        grid=(t // tp,),
        in_specs=cur_specs + halo_specs + [
            pl.BlockSpec((ng, gw, gw), lambda i: (0, 0, 0)),
            pl.BlockSpec((1, pw), lambda i: (0, 0)),
        ],
        out_specs=pl.BlockSpec((tp, pw), lambda i: (i, 0)),
        scratch_shapes=[pltpu.VMEM((HALO + tp, gw), F32)],
        compiler_params=pltpu.CompilerParams(dimension_semantics=("parallel",)),
        name="pool_prompt",
    )(*([kvu] * (2 * ng)), wmix, scale)


def _pool_sample_kernel(ext_ref, wmix_ref, scale_ref, o_ref, *, n_new, gw):
    n_prev = ext_ref.shape[0] - n_new
    for i in range(n_new):
        for g, w in enumerate(POOL_WINDOWS):
            cs = slice(g * gw, (g + 1) * gw)
            u = ext_ref[n_prev + i, :, cs]
            acc = u
            for d in range(1, w):
                acc = acc + ext_ref[n_prev + i - d, :, cs]
            cnt = float(min(w, PAST_LEN + i + 1))
            pooled = acc / cnt - u
            z = jnp.dot(pooled.astype(BF16), wmix_ref[g], preferred_element_type=F32)
            o_ref[i, :, cs] = (z * scale_ref[:, cs]).astype(o_ref.dtype)


def _pool_sample(ext_t, n_new, wmix, scale):
    rows, bd, pw = ext_t.shape
    gw = pw // len(POOL_WINDOWS)
    return pl.pallas_call(
        functools.partial(_pool_sample_kernel, n_new=n_new, gw=gw),
        out_shape=jax.ShapeDtypeStruct((n_new, bd, pw), BF16),
        compiler_params=pltpu.CompilerParams(vmem_limit_bytes=_vmem_limit(2 * rows * bd * pw * 4)),
        name="pool_sample",
    )(ext_t, wmix, scale)


def _merge_route_kernel(attn_ref, pool_ref, sga_ref, sgp_ref, x_ref, wa_ref, wp_ref, wo_ref,
                        gain_ref, wr_ref, br_ref, cnt_in_ref,
                        h_ref, tok_ref, idx_ref, rank_ref, gate_ref, cnt_out_ref, cnt_ref):
    i = pl.program_id(0)
    tm = x_ref.shape[0]
    n_e = wr_ref.shape[1]

    @pl.when(i == 0)
    def _():
        cnt_ref[...] = cnt_in_ref[...]

    a = jnp.dot(attn_ref[...], wa_ref[...], preferred_element_type=F32)
    p = jnp.dot(pool_ref[...], wp_ref[...], preferred_element_type=F32)
    mixed = sga_ref[...].astype(F32) * a + sgp_ref[...].astype(F32) * p
    h = x_ref[...] + jnp.dot(mixed.astype(BF16), wo_ref[...], preferred_element_type=F32)
    h_ref[...] = h
    tok = _rms(h, gain_ref[...])
    tok_ref[...] = tok

    logits = jnp.dot(tok.astype(BF16), wr_ref[...], preferred_element_type=F32) + br_ref[...]
    lane = lax.broadcasted_iota(I32, (tm, n_e), 1).astype(F32)
    work = logits
    member = jnp.zeros((tm, n_e), F32)
    vals, idxs = [], []
    for _ in range(TOP_K):
        mk = jnp.max(work, axis=-1, keepdims=True)
        ik = jnp.min(jnp.where(work == mk, lane, float(n_e)), axis=-1, keepdims=True)
        sel = lane == ik
        vals.append(mk)
        idxs.append(ik)
        member = member + sel.astype(F32)
        work = jnp.where(sel, -jnp.inf, work)
    ex = [jnp.exp(v - vals[0]) for v in vals]
    den = ex[0]
    for e in ex[1:]:
        den = den + e
    rr = lax.broadcasted_iota(I32, (tm, tm), 0)
    cc = lax.broadcasted_iota(I32, (tm, tm), 1)
    tri = (cc < rr).astype(BF16)
    before = jnp.dot(tri, member.astype(BF16), preferred_element_type=F32) + cnt_ref[...]
    ranks = [jnp.sum(jnp.where(lane == ik, before, 0.0), axis=-1, keepdims=True) for ik in idxs]
    cnt_ref[...] = cnt_ref[...] + jnp.sum(member, axis=0, keepdims=True)

    kcol = lax.broadcasted_iota(I32, (tm, TOP_K), 1)

    def columns(cols):
        out = jnp.zeros((tm, TOP_K), F32)
        for k, c in enumerate(cols):
            out = jnp.where(kcol == k, c, out)
        return out

    idx_ref[...] = columns(idxs).astype(I32)
    rank_ref[...] = columns(ranks).astype(I32)
    gate_ref[...] = columns([e / den for e in ex])
    cnt_out_ref[...] = cnt_ref[...]


def _merge_route(attn, pool_o, sg, x, wa, wp, wo, gain, wr, br, cnt_in, tm):
    t, d = x.shape
    qw, pw, n_e = attn.shape[1], pool_o.shape[1], wr.shape[1]
    tm = min(tm, t)
    assert t % tm == 0
    row = lambda w: pl.BlockSpec((tm, w), lambda i: (i, 0))
    whole = lambda a: pl.BlockSpec(a.shape, lambda i: (0,) * a.ndim, pipeline_mode=pl.Buffered(1))
    out_shape = [
        jax.ShapeDtypeStruct((t, d), F32),
        jax.ShapeDtypeStruct((t, d), F32),
        jax.ShapeDtypeStruct((t, TOP_K), I32),
        jax.ShapeDtypeStruct((t, TOP_K), I32),
        jax.ShapeDtypeStruct((t, TOP_K), F32),
        jax.ShapeDtypeStruct((1, n_e), F32),
    ]
    in_specs = [row(qw), row(pw), pl.BlockSpec((tm, d), lambda i: (i, 0)),
                pl.BlockSpec((tm, d), lambda i: (i, 1)), row(d),
                whole(wa), whole(wp), whole(wo), whole(gain), whole(wr), whole(br), whole(cnt_in)]
    est = (wa.size + wp.size + wo.size) * 2 + 2 * tm * (qw + pw + 2 * d) * 2 + 6 * tm * d * 4 + 8 * tm * d * 4
    return pl.pallas_call(
        _merge_route_kernel,
        out_shape=out_shape,
        grid=(t // tm,),
        in_specs=in_specs,
        out_specs=[row(d), row(d), row(TOP_K), row(TOP_K), row(TOP_K),
                   pl.BlockSpec((1, n_e), lambda i: (0, 0))],
        scratch_shapes=[pltpu.VMEM((1, n_e), F32)],
        compiler_params=pltpu.CompilerParams(
            dimension_semantics=("arbitrary",), vmem_limit_bytes=_vmem_limit(est)),
        name="merge_route",
    )(attn, pool_o, sg, sg, x, wa, wp, wo, gain, wr, br, cnt_in)


def _dispatch_kernel(dest_ref, tok_ref, xs_in_ref, xs_ref, sem, *, tm):
    del xs_in_ref
    base = pl.program_id(0) * (tm * TOP_K)

    def row_copy(r, d):
        return pltpu.make_async_copy(tok_ref.at[pl.ds(r, 1)], xs_ref.at[pl.ds(d, 1)], sem)

    def issue(r, c):
        for k in range(TOP_K):
            row_copy(r, dest_ref[base + r * TOP_K + k]).start()
        return c

    lax.fori_loop(0, tm, issue, 0)

    def drain(r, c):
        for k in range(TOP_K):
            row_copy(r, 0).wait()
        return c

    lax.fori_loop(0, tm, drain, 0)


def _dispatch(dest_flat, tok, xs_init, tm):
    t, d = tok.shape
    assert t % tm == 0
    return pl.pallas_call(
        functools.partial(_dispatch_kernel, tm=tm),
        out_shape=jax.ShapeDtypeStruct(xs_init.shape, xs_init.dtype),
        grid_spec=pltpu.PrefetchScalarGridSpec(
            num_scalar_prefetch=1,
            grid=(t // tm,),
            in_specs=[pl.BlockSpec((tm, d), lambda i, dest: (i, 0)),
                      pl.BlockSpec(memory_space=pl.ANY)],
            out_specs=pl.BlockSpec(memory_space=pl.ANY),
            scratch_shapes=[pltpu.SemaphoreType.DMA(())],
        ),
        input_output_aliases={2: 0},
        compiler_params=pltpu.CompilerParams(
            dimension_semantics=("arbitrary",), has_side_effects=True),
        name="dispatch",
    )(dest_flat, tok, xs_init)


def _is_new_expert(be_ref, i):
    return jnp.logical_or(i == 0, be_ref[i] != be_ref[jnp.maximum(i - 1, 0)])


def _expert_up_kernel(be_ref, ok_ref, xs_ref, wg_ref, wu_ref, bg_ref, bu_ref, h_ref, wg_bf, wu_bf):
    i = pl.program_id(1)

    @pl.when(_is_new_expert(be_ref, i))
    def _():
        wg_bf[...] = wg_ref[0].astype(BF16)
        wu_bf[...] = wu_ref[0].astype(BF16)

    @pl.when(ok_ref[i] == 1)
    def _():
        x = xs_ref[...].astype(BF16)
        glu = jnp.dot(x, wg_bf[...], preferred_element_type=F32) + bg_ref[0]
        lin = jnp.dot(x, wu_bf[...], preferred_element_type=F32) + bu_ref[0]
        glu = jnp.minimum(glu, SWIGLU_LIMIT)
        lin = jnp.clip(lin, -SWIGLU_LIMIT, SWIGLU_LIMIT)
        h_ref[...] = (glu * jax.nn.sigmoid(SWIGLU_ALPHA * glu) * (lin + 1.0)).astype(h_ref.dtype)

    @pl.when(ok_ref[i] == 0)
    def _():
        h_ref[...] = jnp.zeros_like(h_ref)


def _expert_up(blk_e, blk_ok, xs, wg, wu, bg, bu, tf):
    ns, d = xs.shape
    n_e, _, dff = wg.shape
    p = EXPERT_ROWS
    tf = min(tf, dff)
    assert ns % p == 0 and dff % tf == 0
    est = 2 * 2 * d * tf * 4 + 2 * d * tf * 2 + 2 * p * d * 4 + 2 * p * tf * 2 + 4 * p * tf * 4
    wspec = pl.BlockSpec((1, d, tf), lambda j, i, be, ok: (be[i], 0, j))
    bspec = pl.BlockSpec((1, 1, tf), lambda j, i, be, ok: (be[i], 0, j))
    return pl.pallas_call(
        _expert_up_kernel,
        out_shape=jax.ShapeDtypeStruct((ns, dff), BF16),
        grid_spec=pltpu.PrefetchScalarGridSpec(
            num_scalar_prefetch=2,
            grid=(dff // tf, ns // p),
            in_specs=[pl.BlockSpec((p, d), lambda j, i, be, ok: (i, 0)), wspec, wspec, bspec, bspec],
            out_specs=pl.BlockSpec((p, tf), lambda j, i, be, ok: (i, j)),
            scratch_shapes=[pltpu.VMEM((d, tf), BF16), pltpu.VMEM((d, tf), BF16)],
        ),
        compiler_params=pltpu.CompilerParams(
            dimension_semantics=("arbitrary", "arbitrary"), vmem_limit_bytes=_vmem_limit(est)),
        name="expert_up",
    )(blk_e, blk_ok, xs, wg, wu, bg.reshape(n_e, 1, dff), bu.reshape(n_e, 1, dff))


def _expert_down_kernel(be_ref, ok_ref, h_ref, wd_ref, bd_ref, y_ref, wd_bf):
    i = pl.program_id(1)

    @pl.when(_is_new_expert(be_ref, i))
    def _():
        wd_bf[...] = wd_ref[0].astype(BF16)

    @pl.when(ok_ref[i] == 1)
    def _():
        y_ref[...] = jnp.dot(h_ref[...], wd_bf[...], preferred_element_type=F32) + bd_ref[0]

    @pl.when(ok_ref[i] == 0)
    def _():
        y_ref[...] = jnp.zeros_like(y_ref)


def _expert_down(blk_e, blk_ok, hmid, wd, bd, tn):
    ns, dff = hmid.shape
    n_e, _, d = wd.shape
    p = EXPERT_ROWS
    tn = min(tn, d)
    assert ns % p == 0 and d % tn == 0
    est = 2 * dff * tn * 4 + dff * tn * 2 + 2 * p * dff * 2 + 2 * p * tn * 4 + 2 * p * tn * 4
    return pl.pallas_call(
        _expert_down_kernel,
        out_shape=jax.ShapeDtypeStruct((ns, d), F32),
        grid_spec=pltpu.PrefetchScalarGridSpec(
            num_scalar_prefetch=2,
            grid=(d // tn, ns // p),
            in_specs=[pl.BlockSpec((p, dff), lambda j, i, be, ok: (i, 0)),
                      pl.BlockSpec((1, dff, tn), lambda j, i, be, ok: (be[i], 0, j)),
                      pl.BlockSpec((1, 1, tn), lambda j, i, be, ok: (be[i], 0, j))],
            out_specs=pl.BlockSpec((p, tn), lambda j, i, be, ok: (i, j)),
            scratch_shapes=[pltpu.VMEM((dff, tn), BF16)],
        ),
        compiler_params=pltpu.CompilerParams(
            dimension_semantics=("arbitrary", "arbitrary"), vmem_limit_bytes=_vmem_limit(est)),
        name="expert_down",
    )(blk_e, blk_ok, hmid, wd, bd.reshape(n_e, 1, d))


def _combine_kernel(dest_ref, h_ref, gate_ref, gain_ref, ys_ref, o_ref, buf_ref, sem, *, tm):
    base = pl.program_id(0) * (tm * TOP_K)

    def row_copy(r, k, d):
        return pltpu.make_async_copy(ys_ref.at[pl.ds(d, 1)], buf_ref.at[k, pl.ds(r, 1)], sem)

    def issue(r, c):
        for k in range(TOP_K):
            row_copy(r, k, dest_ref[base + r * TOP_K + k]).start()
        return c

    lax.fori_loop(0, tm, issue, 0)

    def drain(r, c):
        for k in range(TOP_K):
            row_copy(r, k, 0).wait()
        return c

    lax.fori_loop(0, tm, drain, 0)

    acc = h_ref[...]
    gates = gate_ref[...]
    for k in range(TOP_K):
        acc = acc + gates[:, k:k + 1] * buf_ref[k]
    o_ref[...] = _rms(acc, gain_ref[...])


def _combine(dest_flat, h, gates, gain, ys, tm):
    t, d = h.shape
    tm = min(tm, t)
    assert t % tm == 0
    est = TOP_K * tm * d * 4 + 4 * tm * d * 4 + 2 * tm * d * 4
    return pl.pallas_call(
        functools.partial(_combine_kernel, tm=tm),
        out_shape=jax.ShapeDtypeStruct((t, d), F32),
        grid_spec=pltpu.PrefetchScalarGridSpec(
            num_scalar_prefetch=1,
            grid=(t // tm,),
            in_specs=[pl.BlockSpec((tm, d), lambda i, dest: (i, 0)),
                      pl.BlockSpec((tm, TOP_K), lambda i, dest: (i, 0)),
                      pl.BlockSpec((1, d), lambda i, dest: (0, 0)),
                      pl.BlockSpec(memory_space=pl.ANY)],
            out_specs=pl.BlockSpec((tm, d), lambda i, dest: (i, 0)),
            scratch_shapes=[pltpu.VMEM((TOP_K, tm, d), F32), pltpu.SemaphoreType.DMA(())],
        ),
        compiler_params=pltpu.CompilerParams(
            dimension_semantics=("arbitrary",), vmem_limit_bytes=_vmem_limit(est)),
        name="combine",
    )(dest_flat, h, gates, gain, ys)


def kernel(x_prompt, x_sample, cache_k, cache_v, state_pool, norm_mix, w_in, attn_sinks, w_pool_mix,
           pool_scale, w_attn_out, w_pool_out, w_out, norm_ffn, w_router, b_router, w_gate, b_gate,
           w_up, b_up, w_down, b_down, norm_final):
    assert norm_mix.shape[0] == 1, "single-layer step"
    bp, sp, d = x_prompt.shape
    bs, ns, _ = x_sample.shape
    n_kv, hd = cache_k.shape[-2:]
    n_q = attn_sinks.shape[1]
    qw, kvw = n_q * hd, n_kv * hd
    pw = state_pool.shape[-1]
    n_e = w_router.shape[-1]
    tp_, ts_ = bp * sp, bs * ns
    t_all = tp_ + ts_

    gain_mix = norm_mix[0].reshape(1, d)
    w_in_bf = w_in[0].astype(BF16)
    w_q = w_in_bf[:, :qw]
    w_kvu = w_in_bf[:, qw:qw + 2 * kvw + pw]
    w_g = w_in_bf[:, qw + 2 * kvw + pw:]
    wmix = w_pool_mix[0].astype(BF16)
    pscale = pool_scale[0].reshape(1, pw)
    wa = w_attn_out[0].astype(BF16)
    wp = w_pool_out[0].astype(BF16)
    wo = w_out[0].astype(BF16)
    sinks = attn_sinks[0].astype(F32)

    def project(x2d):
        q = _norm_matmul(x2d, gain_mix, w_q, BF16, "scale", 512, 1024, "proj_q")
        kvu = _norm_matmul(x2d, gain_mix, w_kvu, F32, "none", 512, 768, "proj_kvu")
        sg = _norm_matmul(x2d, gain_mix, w_g, BF16, "sigmoid", 512, 1024, "proj_gates")
        return q, kvu, sg

    xp = x_prompt.reshape(tp_, d)
    q_p, kvu_p, sg_p = project(xp)
    attn_p = _attn_prompt(q_p, kvu_p, sinks, sp, n_kv, hd)
    pool_p = _pool_prompt(kvu_p, 2 * kvw, pw, wmix, pscale, sp, 512)

    xs_ = x_sample.reshape(ts_, d)
    q_s, kvu_s, sg_s = project(xs_)
    k_s = kvu_s[:, :kvw].reshape(bs, ns, kvw)
    v_s = kvu_s[:, kvw:2 * kvw].reshape(bs, ns, kvw)
    u_s = kvu_s[:, 2 * kvw:].reshape(bs, ns, pw)
    ck = cache_k[0].reshape(bs, -1, kvw)
    cv = cache_v[0].reshape(bs, -1, kvw)
    o_s = _attn_sample(q_s.reshape(bs, ns, qw), k_s, v_s, ck, cv, sinks, n_kv, hd)
    attn_s = (o_s.reshape(bs, n_kv, n_q // n_kv, ns, hd).transpose(0, 3, 1, 2, 4)
              .reshape(ts_, qw).astype(BF16))
    ext = jnp.concatenate([state_pool[0], u_s], axis=1)
    pool_s = _pool_sample(ext.transpose(1, 0, 2), ns, wmix, pscale)
    pool_s = pool_s.transpose(1, 0, 2).reshape(ts_, pw)

    gain_ffn = norm_ffn[0].reshape(1, d)
    wr = w_router[0].astype(BF16)
    br = b_router[0].reshape(1, n_e).astype(F32)
    tm = 256
    h_p, tok_p, idx_p, rank_p, gates_p, cnt_p = _merge_route(
        attn_p, pool_p, sg_p, xp, wa, wp, wo, gain_ffn, wr, br, jnp.zeros((1, n_e), F32), tm)
    h_s, tok_s, idx_s, rank_s, gates_s, counts = _merge_route(
        attn_s, pool_s, sg_s, xs_, wa, wp, wo, gain_ffn, wr, br, cnt_p, tm)

    p = EXPERT_ROWS
    cnt = counts[0].astype(I32)
    padded = (cnt + p - 1) // p * p
    pad_end = jnp.cumsum(padded)
    pad_start = pad_end - padded
    dest_p = (pad_start[idx_p] + rank_p).reshape(-1)
    dest_s = (pad_start[idx_s] + rank_s).reshape(-1)
    n_slots = -(-(t_all * TOP_K) // p) * p + n_e * p
    n_tiles = n_slots // p
    tile_start = jnp.arange(n_tiles, dtype=I32) * p
    blk_ok = (tile_start < pad_end[-1]).astype(I32)
    blk_e = jnp.minimum(jnp.searchsorted(pad_end, tile_start, side="right"), n_e - 1).astype(I32)
    blk_e = jnp.where(blk_ok == 1, blk_e, jnp.max(blk_e * blk_ok))

    xs_sorted = _dispatch(dest_p, tok_p, jnp.zeros((n_slots, d), F32), tm)
    xs_sorted = _dispatch(dest_s, tok_s, xs_sorted, tm)
    hmid = _expert_up(blk_e, blk_ok, xs_sorted, w_gate[0], w_up[0], b_gate[0], b_up[0], 1024)
    ys = _expert_down(blk_e, blk_ok, hmid, w_down[0], b_down[0], 1024)

    gain_fin = norm_final.reshape(1, d)
    y_p = _combine(dest_p, h_p, gates_p, gain_fin, ys, tm)
    y_s = _combine(dest_s, h_s, gates_s, gain_fin, ys, tm)

    keep = min(WINDOW, sp)
    k_p4 = kvu_p[:, :kvw].reshape(bp, sp, n_kv, hd)
    v_p4 = kvu_p[:, kvw:2 * kvw].reshape(bp, sp, n_kv, hd)
    u_p3 = kvu_p[:, 2 * kvw:].reshape(bp, sp, pw)
    n_rows = cache_k.shape[2]
    new_k_s = jnp.concatenate([cache_k[0], k_s.reshape(bs, ns, n_kv, hd)], axis=1)[:, -n_rows:]
    new_v_s = jnp.concatenate([cache_v[0], v_s.reshape(bs, ns, n_kv, hd)], axis=1)[:, -n_rows:]
    return (y_p.reshape(bp, sp, d), y_s.reshape(bs, ns, d),
            k_p4[None, :, sp - keep:], v_p4[None, :, sp - keep:], u_p3[None, :, sp - POOL_STATE:],
            new_k_s[None], new_v_s[None], ext[None, :, -POOL_STATE:])
```

```python
import functools

import jax
import jax.numpy as jnp
import numpy as np
from jax import lax
from jax.experimental import pallas as pl
from jax.experimental.pallas import tpu as pltpu

F32 = jnp.float32
BF16 = jnp.bfloat16
I32 = jnp.int32

WINDOW = 128
POOL_WINDOWS = (2, 4, 8, 16)
POOL_STATE = max(POOL_WINDOWS) - 1
PAST_LEN = 16384
TOP_K = 4
SWIGLU_LIMIT = 7.0
SWIGLU_ALPHA = 1.702
RMS_EPS = 1e-5
NEG_BIG = -1e30

V7X_VMEM_BYTES = 64 * 1024 * 1024
EXPERT_ROWS = 256
HALO = 16


def _vmem_limit(nbytes):
    return int(min(nbytes + (8 << 20), V7X_VMEM_BYTES - (4 << 20)))


def _rms(x, gain):
    ms = jnp.mean(x * x, axis=-1, keepdims=True)
    return x * lax.rsqrt(ms + RMS_EPS) * gain


def _rms_cast_kernel(x_ref, g_ref, o_ref):
    o_ref[...] = _rms(x_ref[...], g_ref[...]).astype(o_ref.dtype)


def _rms_cast(x, gain, tm):
    t, d = x.shape
    tm = min(tm, t)
    assert t % tm == 0
    return pl.pallas_call(
        _rms_cast_kernel,
        out_shape=jax.ShapeDtypeStruct((t, d), BF16),
        grid=(t // tm,),
        in_specs=[pl.BlockSpec((tm, d), lambda i: (i, 0)), pl.BlockSpec((1, d), lambda i: (0, 0))],
        out_specs=pl.BlockSpec((tm, d), lambda i: (i, 0)),
        compiler_params=pltpu.CompilerParams(
            dimension_semantics=("parallel",), vmem_limit_bytes=_vmem_limit(16 * tm * d)),
        name="rms_cast",
    )(x, gain)


def _proj_kernel(xn_ref, w_ref, o_ref, *, epilogue):
    acc = jnp.dot(xn_ref[...], w_ref[...], preferred_element_type=F32)
    if epilogue == "sigmoid":
        acc = jax.nn.sigmoid(acc)
    elif epilogue == "scale":
        acc = acc * 0.125
    o_ref[...] = acc.astype(o_ref.dtype)


def _proj(xn, w, out_dtype, epilogue, tm, tn, name):
    t, d = xn.shape
    n = w.shape[1]
    tm = min(tm, t)
    tn = min(tn, n)
    assert t % tm == 0 and n % tn == 0
    est = 2 * tm * d * 2 + 2 * d * tn * 2 + 2 * tm * tn * 4 + 2 * tm * tn * 4
    return pl.pallas_call(
        functools.partial(_proj_kernel, epilogue=epilogue),
        out_shape=jax.ShapeDtypeStruct((t, n), out_dtype),
        grid=(n // tn, t // tm),
        in_specs=[
            pl.BlockSpec((tm, d), lambda j, i: (i, 0)),
            pl.BlockSpec((d, tn), lambda j, i: (0, j)),
        ],
        out_specs=pl.BlockSpec((tm, tn), lambda j, i: (i, j)),
        compiler_params=pltpu.CompilerParams(
            dimension_semantics=("parallel", "parallel"),
            vmem_limit_bytes=_vmem_limit(est)),
        name=name,
    )(xn, w)


def _attn_prompt_kernel(sink_ref, q_ref, kc_ref, vc_ref, kp_ref, vp_ref, o_ref, *,
                        n_kv, q_per_kv, hd, blocks_per_seq, slopes):
    blk = WINDOW
    rq = q_per_kv
    nt = (((1,), (1,)), ((), ()))
    has_prev = (pl.program_id(0) % blocks_per_seq) > 0
    qi = lax.broadcasted_iota(I32, (blk, 2 * blk), 0)
    kj = lax.broadcasted_iota(I32, (blk, 2 * blk), 1)
    dist = qi + blk - kj
    valid = (dist >= 0) & (dist <= WINDOW) & ((kj >= blk) | has_prev)
    distf = dist.astype(F32)

    def scores(g):
        cs = slice(g * hd, (g + 1) * hd)
        k = jnp.concatenate([kp_ref[:, cs], kc_ref[:, cs]], axis=0).astype(BF16)
        qs = jnp.concatenate([q_ref[:, (g * rq + r) * hd:(g * rq + r + 1) * hd] for r in range(rq)], axis=0)
        s = lax.dot_general(qs, k, nt, preferred_element_type=F32)
        slabs, maxes = [], []
        for r in range(rq):
            h = g * rq + r
            sr = jnp.where(valid, s[r * blk:(r + 1) * blk] - slopes[h] * distf, NEG_BIG)
            slabs.append(sr)
            maxes.append(jnp.maximum(jnp.max(sr, axis=-1, keepdims=True), sink_ref[h]))
        return slabs, maxes

    def finish(g, slabs, maxes):
        cs = slice(g * hd, (g + 1) * hd)
        v = jnp.concatenate([vp_ref[:, cs], vc_ref[:, cs]], axis=0).astype(BF16)
        probs, inv = [], []
        for r in range(rq):
            p = jnp.exp(slabs[r] - maxes[r])
            denom = jnp.sum(p, axis=-1, keepdims=True) + jnp.exp(sink_ref[g * rq + r] - maxes[r])
            probs.append(p.astype(BF16))
            inv.append(1.0 / denom)
        o = jnp.dot(jnp.concatenate(probs, axis=0), v, preferred_element_type=F32)
        for r in range(rq):
            h = g * rq + r
            o_ref[:, h * hd:(h + 1) * hd] = (o[r * blk:(r + 1) * blk] * inv[r]).astype(o_ref.dtype)

    pending = scores(0)
    for g in range(n_kv):
        nxt = scores(g + 1) if g + 1 < n_kv else None
        finish(g, *pending)
        pending = nxt


def _attn_prompt(q, kvu, sinks, seq, n_kv, hd):
    t, qw = q.shape
    n_q = qw // hd
    kvw = n_kv * hd
    blk = WINDOW
    assert t % blk == 0 and seq % blk == 0
    slopes = tuple(float(2.0 ** (-8.0 * (h + 1) / n_q)) for h in range(n_q))
    prev = lambda i: jnp.maximum(i - 1, 0)
    return pl.pallas_call(
        functools.partial(_attn_prompt_kernel, n_kv=n_kv, q_per_kv=n_q // n_kv, hd=hd,
                          blocks_per_seq=seq // blk, slopes=slopes),
        out_shape=jax.ShapeDtypeStruct((t, qw), BF16),
        grid=(t // blk,),
        in_specs=[
            pl.BlockSpec(memory_space=pltpu.SMEM),
            pl.BlockSpec((blk, qw), lambda i: (i, 0)),
            pl.BlockSpec((blk, kvw), lambda i: (i, 0)),
            pl.BlockSpec((blk, kvw), lambda i: (i, 1)),
            pl.BlockSpec((blk, kvw), lambda i: (prev(i), 0)),
            pl.BlockSpec((blk, kvw), lambda i: (prev(i), 1)),
        ],
        out_specs=pl.BlockSpec((blk, qw), lambda i: (i, 0)),
        compiler_params=pltpu.CompilerParams(dimension_semantics=("parallel",)),
        name="attn_prompt",
    )(sinks, q, kvu, kvu, kvu, kvu)


def _attn_sample_kernel(qbd_ref, ck_ref, cv_ref, kn_ref, vn_ref, bc_ref, bn_ref, sink_ref, o_ref, *,
                        bb, n_kv, hd):
    nt = (((1,), (1,)), ((), ()))
    rows = qbd_ref.shape[1]
    grp = lax.broadcasted_iota(I32, (rows, 1), 0) // (rows // n_kv)
    sink = sink_ref[...]
    for b in range(bb):
        qb = qbd_ref[b]
        s_c = lax.dot_general(qb, ck_ref[b].astype(BF16), nt, preferred_element_type=F32) + bc_ref[...]
        s_n = lax.dot_general(qb, kn_ref[b].astype(BF16), nt, preferred_element_type=F32) + bn_ref[...]
        m = jnp.maximum(jnp.maximum(jnp.max(s_c, axis=-1, keepdims=True),
                                    jnp.max(s_n, axis=-1, keepdims=True)), sink)
        p_c = jnp.exp(s_c - m)
        p_n = jnp.exp(s_n - m)
        denom = (jnp.sum(p_c, axis=-1, keepdims=True) + jnp.sum(p_n, axis=-1, keepdims=True)
                 + jnp.exp(sink - m))
        o = (jnp.dot(p_c.astype(BF16), cv_ref[b].astype(BF16), preferred_element_type=F32)
             + jnp.dot(p_n.astype(BF16), vn_ref[b].astype(BF16), preferred_element_type=F32))
        sel = jnp.zeros((rows, hd), F32)
        for g in range(n_kv):
            sel = sel + jnp.where(grp == g, o[:, g * hd:(g + 1) * hd], 0.0)
        o_ref[b] = sel / denom


def _attn_sample(q_s, k_new, v_new, cache_k, cache_v, sinks, n_kv, hd):
    bd, n, qw = q_s.shape
    n_q = qw // hd
    r = n_q // n_kv
    kvw = n_kv * hd
    w = cache_k.shape[1]
    rows = n_q * n
    npad = 16
    q5 = q_s.reshape(bd, n, n_kv, r, hd)
    qbd = jnp.einsum("bigrd,gh->bgrihd", q5, jnp.eye(n_kv, dtype=q_s.dtype)).reshape(bd, rows, kvw)
    pad = ((0, 0), (0, npad - n), (0, 0))
    k_new = jnp.pad(k_new, pad)
    v_new = jnp.pad(v_new, pad)
    slopes = 2.0 ** (-8.0 * np.arange(1, n_q + 1, dtype=np.float64) / n_q)
    slope_c = np.repeat(slopes, n)
    i_c = np.tile(np.arange(n), n_q)
    dist_c = (w + i_c)[:, None] - np.arange(w)[None, :]
    bias_c = np.where((dist_c >= 0) & (dist_c <= WINDOW), -slope_c[:, None] * dist_c, NEG_BIG)
    dist_n = i_c[:, None] - np.arange(npad)[None, :]
    ok_n = (dist_n >= 0) & (np.arange(npad)[None, :] < n)
    bias_n = np.where(ok_n, -slope_c[:, None] * dist_n, NEG_BIG)
    sink_c = jnp.repeat(sinks.astype(F32), n).reshape(rows, 1)
    bb = 8 if bd % 8 == 0 else 1
    return pl.pallas_call(
        functools.partial(_attn_sample_kernel, bb=bb, n_kv=n_kv, hd=hd),
        out_shape=jax.ShapeDtypeStruct((bd, rows, hd), F32),
        grid=(bd // bb,),
        in_specs=[
            pl.BlockSpec((bb, rows, kvw), lambda i: (i, 0, 0)),
            pl.BlockSpec((bb, w, kvw), lambda i: (i, 0, 0)),
            pl.BlockSpec((bb, w, kvw), lambda i: (i, 0, 0)),
            pl.BlockSpec((bb, npad, kvw), lambda i: (i, 0, 0)),
            pl.BlockSpec((bb, npad, kvw), lambda i: (i, 0, 0)),
            pl.BlockSpec((rows, w), lambda i: (0, 0)),
            pl.BlockSpec((rows, npad), lambda i: (0, 0)),
            pl.BlockSpec((rows, 1), lambda i: (0, 0)),
        ],
        out_specs=pl.BlockSpec((bb, rows, hd), lambda i: (i, 0, 0)),
        compiler_params=pltpu.CompilerParams(dimension_semantics=("parallel",)),
        name="attn_sample",
    )(qbd, cache_k, cache_v, k_new, v_new,
      jnp.asarray(bias_c, F32), jnp.asarray(bias_n, F32), sink_c)


def _pool_prompt_kernel(*refs, tp, gw, tiles_per_seq):
    ng = len(POOL_WINDOWS)
    cur = refs[:ng]
    halo = refs[ng:2 * ng]
    wmix_ref, scale_ref, o_ref, ext_ref = refs[2 * ng:]
    tile_in_seq = pl.program_id(0) % tiles_per_seq
    first = tile_in_seq == 0
    pos = lax.broadcasted_iota(I32, (tp, 1), 0) + tile_in_seq * tp
    for g, w in enumerate(POOL_WINDOWS):
        u = cur[g][...]
        ext_ref[0:HALO, :] = jnp.where(first, 0.0, halo[g][...])
        ext_ref[HALO:HALO + tp, :] = u
        acc = u
        for d in range(1, w):
            acc = acc + ext_ref[HALO - d:HALO - d + tp, :]
        cnt = jnp.minimum(w, pos + 1).astype(F32)
        pooled = acc / cnt - u
        z = jnp.dot(pooled.astype(BF16), wmix_ref[g], preferred_element_type=F32)
        o_ref[:, g * gw:(g + 1) * gw] = (z * scale_ref[:, g * gw:(g + 1) * gw]).astype(o_ref.dtype)


def _pool_prompt(kvu, u_col0, pw, wmix, scale, seq, tp):
    t = kvu.shape[0]
    ng = len(POOL_WINDOWS)
    gw = pw // ng
    tp = min(tp, seq)
    assert seq % tp == 0 and tp % HALO == 0 and u_col0 % gw == 0
    c0 = u_col0 // gw
    hb = tp // HALO
    cur_specs = [pl.BlockSpec((tp, gw), functools.partial(lambda i, g: (i, c0 + g), g=g)) for g in range(ng)]
    halo_specs = [pl.BlockSpec((HALO, gw),
                               functools.partial(lambda i, g: (jnp.maximum(i * hb - 1, 0), c0 + g), g=g))
                  for g in range(ng)]
    return pl.pallas_call(
        functools.partial(_pool_prompt_kernel, tp=tp, gw=gw, tiles_per_seq=seq // tp),
        out_shape=jax.ShapeDtypeStruct((t, pw), BF16),
        grid=(t // tp,),
        in_specs=cur_specs + halo_specs + [
            pl.BlockSpec((ng, gw, gw), lambda i: (0, 0, 0)),
            pl.BlockSpec((1, pw), lambda i: (0, 0)),
        ],
        out_specs=pl.BlockSpec((tp, pw), lambda i: (i, 0)),
        scratch_shapes=[pltpu.VMEM((HALO + tp, gw), F32)],
        compiler_params=pltpu.CompilerParams(dimension_semantics=("parallel",)),
        name="pool_prompt",
    )(*([kvu] * (2 * ng)), wmix, scale)


def _pool_sample_kernel(ext_ref, wmix_ref, scale_ref, o_ref, *, n_new, gw):
    n_prev = ext_ref.shape[0] - n_new
    for i in range(n_new):
        for g, w in enumerate(POOL_WINDOWS):
            cs = slice(g * gw, (g + 1) * gw)
            u = ext_ref[n_prev + i, :, cs]
            acc = u
            for d in range(1, w):
                acc = acc + ext_ref[n_prev + i - d, :, cs]
            cnt = float(min(w, PAST_LEN + i + 1))
            pooled = acc / cnt - u
            z = jnp.dot(pooled.astype(BF16), wmix_ref[g], preferred_element_type=F32)
            o_ref[i, :, cs] = (z * scale_ref[:, cs]).astype(o_ref.dtype)


def _pool_sample(ext_t, n_new, wmix, scale):
    rows, bd, pw = ext_t.shape
    gw = pw // len(POOL_WINDOWS)
    return pl.pallas_call(
        functools.partial(_pool_sample_kernel, n_new=n_new, gw=gw),
        out_shape=jax.ShapeDtypeStruct((n_new, bd, pw), BF16),
        compiler_params=pltpu.CompilerParams(vmem_limit_bytes=_vmem_limit(2 * rows * bd * pw * 4)),
        name="pool_sample",
    )(ext_t, wmix, scale)


def _merge_route_kernel(attn_ref, pool_ref, sga_ref, sgp_ref, x_ref, wa_ref, wp_ref, wo_ref,
                        gain_ref, wr_ref, br_ref, cnt_in_ref,
                        h_ref, tok_ref, idx_ref, rank_ref, gate_ref, cnt_out_ref, cnt_ref):
    i = pl.program_id(0)
    tm = x_ref.shape[0]
    n_e = wr_ref.shape[0]

    @pl.when(i == 0)
    def _():
        cnt_ref[...] = cnt_in_ref[...]

    a = jnp.dot(attn_ref[...], wa_ref[...], preferred_element_type=F32)
    p = jnp.dot(pool_ref[...], wp_ref[...], preferred_element_type=F32)
    mixed = sga_ref[...].astype(F32) * a + sgp_ref[...].astype(F32) * p
    h = x_ref[...] + jnp.dot(mixed.astype(BF16), wo_ref[...], preferred_element_type=F32)
    h_ref[...] = h
    tok = _rms(h, gain_ref[...])
    tok_ref[...] = tok

    logits = lax.dot_general(wr_ref[...], tok.astype(BF16), (((1,), (1,)), ((), ())),
                             preferred_element_type=F32) + br_ref[...]
    eid = lax.broadcasted_iota(I32, (n_e, tm), 0).astype(F32)
    work = logits
    member = jnp.zeros((n_e, tm), F32)
    vals, idxs = [], []
    for _ in range(TOP_K):
        mk = jnp.max(work, axis=0, keepdims=True)
        ik = jnp.min(jnp.where(work == mk, eid, float(n_e)), axis=0, keepdims=True)
        sel = eid == ik
        vals.append(mk)
        idxs.append(ik)
        member = member + sel.astype(F32)
        work = jnp.where(sel, -jnp.inf, work)
    ex = [jnp.exp(v - vals[0]) for v in vals]
    den = ex[0]
    for e in ex[1:]:
        den = den + e
    rr = lax.broadcasted_iota(I32, (tm, tm), 0)
    cc = lax.broadcasted_iota(I32, (tm, tm), 1)
    earlier = (rr < cc).astype(BF16)
    before = jnp.dot(member.astype(BF16), earlier, preferred_element_type=F32) + cnt_ref[...]
    ranks = [jnp.sum(jnp.where(eid == ik, before, 0.0), axis=0, keepdims=True) for ik in idxs]
    cnt_ref[...] = cnt_ref[...] + jnp.sum(member, axis=1, keepdims=True)

    krow = lax.broadcasted_iota(I32, (TOP_K, tm), 0)

    def rows(parts):
        out = jnp.zeros((TOP_K, tm), F32)
        for k, c in enumerate(parts):
            out = jnp.where(krow == k, c, out)
        return out

    idx_ref[...] = rows(idxs).astype(I32)
    rank_ref[...] = rows(ranks).astype(I32)
    gate_ref[...] = rows([e / den for e in ex])
    cnt_out_ref[...] = cnt_ref[...]


def _merge_route(attn, pool_o, sg, x, wa, wp, wo, gain, wr, br, cnt_in, tm):
    t, d = x.shape
    qw, pw, n_e = attn.shape[1], pool_o.shape[1], wr.shape[0]
    tm = min(tm, t)
    assert t % tm == 0
    per_tok = pl.BlockSpec((TOP_K, tm), lambda i: (0, i))
    row = lambda w: pl.BlockSpec((tm, w), lambda i: (i, 0))
    whole = lambda a: pl.BlockSpec(a.shape, lambda i: (0,) * a.ndim, pipeline_mode=pl.Buffered(1))
    out_shape = [
        jax.ShapeDtypeStruct((t, d), F32),
        jax.ShapeDtypeStruct((t, d), F32),
        jax.ShapeDtypeStruct((TOP_K, t), I32),
        jax.ShapeDtypeStruct((TOP_K, t), I32),
        jax.ShapeDtypeStruct((TOP_K, t), F32),
        jax.ShapeDtypeStruct((n_e, 1), F32),
    ]
    in_specs = [row(qw), row(pw), pl.BlockSpec((tm, d), lambda i: (i, 0)),
                pl.BlockSpec((tm, d), lambda i: (i, 1)), row(d),
                whole(wa), whole(wp), whole(wo), whole(gain), whole(wr), whole(br), whole(cnt_in)]
    est = (wa.size + wp.size + wo.size) * 2 + 2 * tm * (qw + pw + 2 * d) * 2 + 6 * tm * d * 4 + 8 * tm * d * 4
    return pl.pallas_call(
        _merge_route_kernel,
        out_shape=out_shape,
        grid=(t // tm,),
        in_specs=in_specs,
        out_specs=[row(d), row(d), per_tok, per_tok, per_tok,
                   pl.BlockSpec((n_e, 1), lambda i: (0, 0))],
        scratch_shapes=[pltpu.VMEM((n_e, 1), F32)],
        compiler_params=pltpu.CompilerParams(
            dimension_semantics=("arbitrary",), vmem_limit_bytes=_vmem_limit(est)),
        name="merge_route",
    )(attn, pool_o, sg, sg, x, wa, wp, wo, gain, wr, br, cnt_in)


def _dispatch_kernel(dest_ref, tok_ref, xs_in_ref, xs_ref, sem, *, tm, t):
    del xs_in_ref
    base = pl.program_id(0) * tm

    def row_copy(r, d):
        return pltpu.make_async_copy(tok_ref.at[pl.ds(r, 1)], xs_ref.at[pl.ds(d, 1)], sem)

    def issue(r, c):
        for k in range(TOP_K):
            row_copy(r, dest_ref[k * t + base + r]).start()
        return c

    lax.fori_loop(0, tm, issue, 0)

    def drain(r, c):
        for k in range(TOP_K):
            row_copy(r, 0).wait()
        return c

    lax.fori_loop(0, tm, drain, 0)


def _dispatch(dest_flat, tok, xs_init, tm):
    t, d = tok.shape
    tm = min(tm, t)
    assert t % tm == 0
    return pl.pallas_call(
        functools.partial(_dispatch_kernel, tm=tm, t=t),
        out_shape=jax.ShapeDtypeStruct(xs_init.shape, xs_init.dtype),
        grid_spec=pltpu.PrefetchScalarGridSpec(
            num_scalar_prefetch=1,
            grid=(t // tm,),
            in_specs=[pl.BlockSpec((tm, d), lambda i, dest: (i, 0)),
                      pl.BlockSpec(memory_space=pl.ANY)],
            out_specs=pl.BlockSpec(memory_space=pl.ANY),
            scratch_shapes=[pltpu.SemaphoreType.DMA(())],
        ),
        input_output_aliases={2: 0},
        compiler_params=pltpu.CompilerParams(
            dimension_semantics=("arbitrary",), has_side_effects=True),
        name="dispatch",
    )(dest_flat, tok, xs_init)


def _is_new_expert(be_ref, i):
    return jnp.logical_or(i == 0, be_ref[i] != be_ref[jnp.maximum(i - 1, 0)])


def _expert_up_kernel(be_ref, ok_ref, xs_ref, wg_ref, wu_ref, bg_ref, bu_ref, h_ref, wg_bf, wu_bf):
    i = pl.program_id(1)

    @pl.when(_is_new_expert(be_ref, i))
    def _():
        wg_bf[...] = wg_ref[0].astype(BF16)
        wu_bf[...] = wu_ref[0].astype(BF16)

    @pl.when(ok_ref[i] == 1)
    def _():
        x = xs_ref[...].astype(BF16)
        tf = h_ref.shape[1]
        cw = min(tf, 512)
        for c in range(tf // cw):
            cs = slice(c * cw, (c + 1) * cw)
            glu = jnp.dot(x, wg_bf[:, cs], preferred_element_type=F32) + bg_ref[0, :, cs]
            lin = jnp.dot(x, wu_bf[:, cs], preferred_element_type=F32) + bu_ref[0, :, cs]
            glu = jnp.minimum(glu, SWIGLU_LIMIT)
            lin = jnp.clip(lin, -SWIGLU_LIMIT, SWIGLU_LIMIT)
            h_ref[:, cs] = (glu * jax.nn.sigmoid(SWIGLU_ALPHA * glu) * (lin + 1.0)).astype(h_ref.dtype)

    @pl.when(ok_ref[i] == 0)
    def _():
        h_ref[...] = jnp.zeros_like(h_ref)


def _expert_up(blk_e, blk_ok, xs, wg, wu, bg, bu, tf):
    ns, d = xs.shape
    n_e, _, dff = wg.shape
    p = EXPERT_ROWS
    tf = min(tf, dff)
    assert ns % p == 0 and dff % tf == 0
    est = 2 * 2 * d * tf * 4 + 2 * d * tf * 2 + 2 * p * d * 4 + 2 * p * tf * 2 + 4 * p * tf * 4
    wspec = pl.BlockSpec((1, d, tf), lambda j, i, be, ok: (be[i], 0, j))
    bspec = pl.BlockSpec((1, 1, tf), lambda j, i, be, ok: (be[i], 0, j))
    return pl.pallas_call(
        _expert_up_kernel,
        out_shape=jax.ShapeDtypeStruct((ns, dff), BF16),
        grid_spec=pltpu.PrefetchScalarGridSpec(
            num_scalar_prefetch=2,
            grid=(dff // tf, ns // p),
            in_specs=[pl.BlockSpec((p, d), lambda j, i, be, ok: (i, 0)), wspec, wspec, bspec, bspec],
            out_specs=pl.BlockSpec((p, tf), lambda j, i, be, ok: (i, j)),
            scratch_shapes=[pltpu.VMEM((d, tf), BF16), pltpu.VMEM((d, tf), BF16)],
        ),
        compiler_params=pltpu.CompilerParams(
            dimension_semantics=("arbitrary", "arbitrary"), vmem_limit_bytes=_vmem_limit(est)),
        name="expert_up",
    )(blk_e, blk_ok, xs, wg, wu, bg.reshape(n_e, 1, dff), bu.reshape(n_e, 1, dff))


def _expert_down_kernel(be_ref, ok_ref, h_ref, wd_ref, bd_ref, y_ref, wd_bf):
    i = pl.program_id(1)

    @pl.when(_is_new_expert(be_ref, i))
    def _():
        wd_bf[...] = wd_ref[0].astype(BF16)

    @pl.when(ok_ref[i] == 1)
    def _():
        y_ref[...] = jnp.dot(h_ref[...], wd_bf[...], preferred_element_type=F32) + bd_ref[0]

    @pl.when(ok_ref[i] == 0)
    def _():
        y_ref[...] = jnp.zeros_like(y_ref)


def _expert_down(blk_e, blk_ok, hmid, wd, bd, tn):
    ns, dff = hmid.shape
    n_e, _, d = wd.shape
    p = EXPERT_ROWS
    tn = min(tn, d)
    assert ns % p == 0 and d % tn == 0
    est = 2 * dff * tn * 4 + dff * tn * 2 + 2 * p * dff * 2 + 2 * p * tn * 4 + 2 * p * tn * 4
    return pl.pallas_call(
        _expert_down_kernel,
        out_shape=jax.ShapeDtypeStruct((ns, d), F32),
        grid_spec=pltpu.PrefetchScalarGridSpec(
            num_scalar_prefetch=2,
            grid=(d // tn, ns // p),
            in_specs=[pl.BlockSpec((p, dff), lambda j, i, be, ok: (i, 0)),
                      pl.BlockSpec((1, dff, tn), lambda j, i, be, ok: (be[i], 0, j)),
                      pl.BlockSpec((1, 1, tn), lambda j, i, be, ok: (be[i], 0, j))],
            out_specs=pl.BlockSpec((p, tn), lambda j, i, be, ok: (i, j)),
            scratch_shapes=[pltpu.VMEM((dff, tn), BF16)],
        ),
        compiler_params=pltpu.CompilerParams(
            dimension_semantics=("arbitrary", "arbitrary"), vmem_limit_bytes=_vmem_limit(est)),
        name="expert_down",
    )(blk_e, blk_ok, hmid, wd, bd.reshape(n_e, 1, d))


def _combine_kernel(dest_ref, h_ref, gate_ref, gain_ref, ys_ref, o_ref, buf_ref, sem, *, tm, t):
    base = pl.program_id(0) * tm

    def row_copy(r, k, d):
        return pltpu.make_async_copy(ys_ref.at[pl.ds(d, 1)], buf_ref.at[k, pl.ds(r, 1)], sem)

    def issue(r, c):
        for k in range(TOP_K):
            row_copy(r, k, dest_ref[k * t + base + r]).start()
        return c

    lax.fori_loop(0, tm, issue, 0)

    def drain(r, c):
        for k in range(TOP_K):
            row_copy(r, k, 0).wait()
        return c

    lax.fori_loop(0, tm, drain, 0)

    acc = h_ref[...]
    gates = gate_ref[...]
    for k in range(TOP_K):
        acc = acc + gates[:, k:k + 1] * buf_ref[k]
    o_ref[...] = _rms(acc, gain_ref[...])


def _combine(dest_flat, h, gates, gain, ys, tm):
    t, d = h.shape
    tm = min(tm, t)
    assert t % tm == 0
    est = TOP_K * tm * d * 4 + 4 * tm * d * 4 + 2 * tm * d * 4
    return pl.pallas_call(
        functools.partial(_combine_kernel, tm=tm, t=t),
        out_shape=jax.ShapeDtypeStruct((t, d), F32),
        grid_spec=pltpu.PrefetchScalarGridSpec(
            num_scalar_prefetch=1,
            grid=(t // tm,),
            in_specs=[pl.BlockSpec((tm, d), lambda i, dest: (i, 0)),
                      pl.BlockSpec((tm, TOP_K), lambda i, dest: (i, 0)),
                      pl.BlockSpec((1, d), lambda i, dest: (0, 0)),
                      pl.BlockSpec(memory_space=pl.ANY)],
            out_specs=pl.BlockSpec((tm, d), lambda i, dest: (i, 0)),
            scratch_shapes=[pltpu.VMEM((TOP_K, tm, d), F32), pltpu.SemaphoreType.DMA(())],
        ),
        compiler_params=pltpu.CompilerParams(
            dimension_semantics=("arbitrary",), vmem_limit_bytes=_vmem_limit(est)),
        name="combine",
    )(dest_flat, h, gates, gain, ys)


def kernel(x_prompt, x_sample, cache_k, cache_v, state_pool, norm_mix, w_in, attn_sinks, w_pool_mix,
           pool_scale, w_attn_out, w_pool_out, w_out, norm_ffn, w_router, b_router, w_gate, b_gate,
           w_up, b_up, w_down, b_down, norm_final):
    assert norm_mix.shape[0] == 1, "single-layer step"
    bp, sp, d = x_prompt.shape
    bs, ns, _ = x_sample.shape
    n_kv, hd = cache_k.shape[-2:]
    n_q = attn_sinks.shape[1]
    qw, kvw = n_q * hd, n_kv * hd
    pw = state_pool.shape[-1]
    n_e = w_router.shape[-1]
    tp_, ts_ = bp * sp, bs * ns
    t_all = tp_ + ts_

    gain_mix = norm_mix[0].reshape(1, d)
    w_in_bf = w_in[0].astype(BF16)
    w_q = w_in_bf[:, :qw]
    w_kvu = w_in_bf[:, qw:qw + 2 * kvw + pw]
    w_g = w_in_bf[:, qw + 2 * kvw + pw:]
    wmix = w_pool_mix[0].astype(BF16)
    pscale = pool_scale[0].reshape(1, pw)
    wa = w_attn_out[0].astype(BF16)
    wp = w_pool_out[0].astype(BF16)
    wo = w_out[0].astype(BF16)
    sinks = attn_sinks[0].astype(F32)

    def project(x2d):
        xn = _rms_cast(x2d, gain_mix, 512)
        q = _proj(xn, w_q, BF16, "scale", 1024, 1024, "proj_q")
        kvu = _proj(xn, w_kvu, F32, "none", 1024, 768, "proj_kvu")
        sg = _proj(xn, w_g, BF16, "sigmoid", 1024, 1024, "proj_gates")
        return q, kvu, sg

    xp = x_prompt.reshape(tp_, d)
    q_p, kvu_p, sg_p = project(xp)
    attn_p = _attn_prompt(q_p, kvu_p, sinks, sp, n_kv, hd)
    pool_p = _pool_prompt(kvu_p, 2 * kvw, pw, wmix, pscale, sp, 512)

    xs_ = x_sample.reshape(ts_, d)
    q_s, kvu_s, sg_s = project(xs_)
    k_s = kvu_s[:, :kvw].reshape(bs, ns, kvw)
    v_s = kvu_s[:, kvw:2 * kvw].reshape(bs, ns, kvw)
    u_s = kvu_s[:, 2 * kvw:].reshape(bs, ns, pw)
    ck = cache_k[0].reshape(bs, -1, kvw)
    cv = cache_v[0].reshape(bs, -1, kvw)
    o_s = _attn_sample(q_s.reshape(bs, ns, qw), k_s, v_s, ck, cv, sinks, n_kv, hd)
    attn_s = (o_s.reshape(bs, n_kv, n_q // n_kv, ns, hd).transpose(0, 3, 1, 2, 4)
              .reshape(ts_, qw).astype(BF16))
    ext = jnp.concatenate([state_pool[0], u_s], axis=1)
    pool_s = _pool_sample(ext.transpose(1, 0, 2), ns, wmix, pscale)
    pool_s = pool_s.transpose(1, 0, 2).reshape(ts_, pw)

    gain_ffn = norm_ffn[0].reshape(1, d)
    wr = w_router[0].T.astype(BF16)
    br = b_router[0].reshape(n_e, 1).astype(F32)
    tm = 256
    h_p, tok_p, idx_p, rank_p, gates_p, cnt_p = _merge_route(
        attn_p, pool_p, sg_p, xp, wa, wp, wo, gain_ffn, wr, br, jnp.zeros((n_e, 1), F32), tm)
    h_s, tok_s, idx_s, rank_s, gates_s, counts = _merge_route(
        attn_s, pool_s, sg_s, xs_, wa, wp, wo, gain_ffn, wr, br, cnt_p, tm)

    p = EXPERT_ROWS
    cnt = counts[:, 0].astype(I32)
    padded = (cnt + p - 1) // p * p
    pad_end = jnp.cumsum(padded)
    pad_start = pad_end - padded
    dest_p = (pad_start[idx_p] + rank_p).reshape(-1)
    dest_s = (pad_start[idx_s] + rank_s).reshape(-1)
    n_slots = -(-(t_all * TOP_K) // p) * p + n_e * p
    n_tiles = n_slots // p
    tile_start = jnp.arange(n_tiles, dtype=I32) * p
    blk_ok = (tile_start < pad_end[-1]).astype(I32)
    blk_e = jnp.sum((pad_end[None, :] <= tile_start[:, None]).astype(I32), axis=1)
    blk_e = jnp.minimum(blk_e, n_e - 1)
    blk_e = jnp.where(blk_ok == 1, blk_e, jnp.max(blk_e * blk_ok))

    xs_sorted = _dispatch(dest_p, tok_p, jnp.zeros((n_slots, d), F32), tm)
    xs_sorted = _dispatch(dest_s, tok_s, xs_sorted, tm)
    hmid = _expert_up(blk_e, blk_ok, xs_sorted, w_gate[0], w_up[0], b_gate[0], b_up[0], 1024)
    ys = _expert_down(blk_e, blk_ok, hmid, w_down[0], b_down[0], 2048)

    gain_fin = norm_final.reshape(1, d)
    y_p = _combine(dest_p, h_p, gates_p.T, gain_fin, ys, tm)
    y_s = _combine(dest_s, h_s, gates_s.T, gain_fin, ys, tm)

    keep = min(WINDOW, sp)
    tail_p = kvu_p.reshape(bp, sp, 2 * kvw + pw)[:, sp - keep:]
    n_rows = cache_k.shape[2]
    new_k_s = jnp.concatenate([cache_k[0], k_s.reshape(bs, ns, n_kv, hd)], axis=1)[:, -n_rows:]
    new_v_s = jnp.concatenate([cache_v[0], v_s.reshape(bs, ns, n_kv, hd)], axis=1)[:, -n_rows:]
    return (y_p.reshape(bp, sp, d), y_s.reshape(bs, ns, d),
            tail_p[:, :, :kvw].reshape(1, bp, keep, n_kv, hd),
            tail_p[:, :, kvw:2 * kvw].reshape(1, bp, keep, n_kv, hd),
            tail_p[None, :, keep - POOL_STATE:, 2 * kvw:],
            new_k_s[None], new_v_s[None], ext[None, :, -POOL_STATE:])
```

```python
import functools

import jax
import jax.numpy as jnp
import numpy as np
from jax import lax
from jax.experimental import pallas as pl
from jax.experimental.pallas import tpu as pltpu

F32 = jnp.float32
BF16 = jnp.bfloat16
I32 = jnp.int32

WINDOW = 128
POOL_WINDOWS = (2, 4, 8, 16)
POOL_STATE = max(POOL_WINDOWS) - 1
PAST_LEN = 16384
TOP_K = 4
SWIGLU_LIMIT = 7.0
SWIGLU_ALPHA = 1.702
RMS_EPS = 1e-5
NEG_BIG = -1e30

V7X_VMEM_BYTES = 64 * 1024 * 1024
EXPERT_ROWS = 256
HALO = 16


def _vmem_limit(nbytes):
    return int(min(nbytes + (8 << 20), V7X_VMEM_BYTES - (4 << 20)))


def _rms(x, gain):
    ms = jnp.mean(x * x, axis=-1, keepdims=True)
    return x * lax.rsqrt(ms + RMS_EPS) * gain


def _rms_cast_kernel(x_ref, g_ref, o_ref):
    o_ref[...] = _rms(x_ref[...], g_ref[...]).astype(o_ref.dtype)


def _rms_cast(x, gain, tm):
    t, d = x.shape
    tm = min(tm, t)
    assert t % tm == 0
    return pl.pallas_call(
        _rms_cast_kernel,
        out_shape=jax.ShapeDtypeStruct((t, d), BF16),
        grid=(t // tm,),
        in_specs=[pl.BlockSpec((tm, d), lambda i: (i, 0)), pl.BlockSpec((1, d), lambda i: (0, 0))],
        out_specs=pl.BlockSpec((tm, d), lambda i: (i, 0)),
        compiler_params=pltpu.CompilerParams(
            dimension_semantics=("parallel",), vmem_limit_bytes=_vmem_limit(16 * tm * d)),
        name="rms_cast",
    )(x, gain)


def _proj_kernel(xn_ref, w_ref, o_ref, *, epilogue):
    acc = jnp.dot(xn_ref[...], w_ref[...], preferred_element_type=F32)
    if epilogue == "sigmoid":
        acc = jax.nn.sigmoid(acc)
    elif epilogue == "scale":
        acc = acc * 0.125
    o_ref[...] = acc.astype(o_ref.dtype)


def _proj(xn, w, out_dtype, epilogue, tm, tn, name):
    t, d = xn.shape
    n = w.shape[1]
    tm = min(tm, t)
    tn = min(tn, n)
    assert t % tm == 0 and n % tn == 0
    est = 2 * tm * d * 2 + 2 * d * tn * 2 + 2 * tm * tn * 4 + 2 * tm * tn * 4
    return pl.pallas_call(
        functools.partial(_proj_kernel, epilogue=epilogue),
        out_shape=jax.ShapeDtypeStruct((t, n), out_dtype),
        grid=(n // tn, t // tm),
        in_specs=[
            pl.BlockSpec((tm, d), lambda j, i: (i, 0)),
            pl.BlockSpec((d, tn), lambda j, i: (0, j)),
        ],
        out_specs=pl.BlockSpec((tm, tn), lambda j, i: (i, j)),
        compiler_params=pltpu.CompilerParams(
            dimension_semantics=("parallel", "parallel"),
            vmem_limit_bytes=_vmem_limit(est)),
        name=name,
    )(xn, w)


def _attn_prompt_kernel(sink_ref, q_ref, kc_ref, vc_ref, kp_ref, vp_ref, o_ref, *,
                        n_kv, q_per_kv, hd, blocks_per_seq, slopes):
    blk = WINDOW
    rq = q_per_kv
    nt = (((1,), (1,)), ((), ()))
    has_prev = (pl.program_id(0) % blocks_per_seq) > 0
    qi = lax.broadcasted_iota(I32, (blk, 2 * blk), 0)
    kj = lax.broadcasted_iota(I32, (blk, 2 * blk), 1)
    dist = qi + blk - kj
    valid = (dist >= 0) & (dist <= WINDOW) & ((kj >= blk) | has_prev)
    distf = dist.astype(F32)

    def scores(g):
        cs = slice(g * hd, (g + 1) * hd)
        k = jnp.concatenate([kp_ref[:, cs], kc_ref[:, cs]], axis=0).astype(BF16)
        qs = jnp.concatenate([q_ref[:, (g * rq + r) * hd:(g * rq + r + 1) * hd] for r in range(rq)], axis=0)
        s = lax.dot_general(qs, k, nt, preferred_element_type=F32)
        slabs, maxes = [], []
        for r in range(rq):
            h = g * rq + r
            sr = jnp.where(valid, s[r * blk:(r + 1) * blk] - slopes[h] * distf, NEG_BIG)
            slabs.append(sr)
            maxes.append(jnp.maximum(jnp.max(sr, axis=-1, keepdims=True), sink_ref[h]))
        return slabs, maxes

    def finish(g, slabs, maxes):
        cs = slice(g * hd, (g + 1) * hd)
        v = jnp.concatenate([vp_ref[:, cs], vc_ref[:, cs]], axis=0).astype(BF16)
        probs, inv = [], []
        for r in range(rq):
            p = jnp.exp(slabs[r] - maxes[r])
            denom = jnp.sum(p, axis=-1, keepdims=True) + jnp.exp(sink_ref[g * rq + r] - maxes[r])
            probs.append(p.astype(BF16))
            inv.append(1.0 / denom)
        o = jnp.dot(jnp.concatenate(probs, axis=0), v, preferred_element_type=F32)
        for r in range(rq):
            h = g * rq + r
            o_ref[:, h * hd:(h + 1) * hd] = (o[r * blk:(r + 1) * blk] * inv[r]).astype(o_ref.dtype)

    pending = scores(0)
    for g in range(n_kv):
        nxt = scores(g + 1) if g + 1 < n_kv else None
        finish(g, *pending)
        pending = nxt


def _attn_prompt(q, kvu, sinks, seq, n_kv, hd):
    t, qw = q.shape
    n_q = qw // hd
    kvw = n_kv * hd
    blk = WINDOW
    assert t % blk == 0 and seq % blk == 0
    slopes = tuple(float(2.0 ** (-8.0 * (h + 1) / n_q)) for h in range(n_q))
    prev = lambda i: jnp.maximum(i - 1, 0)
    return pl.pallas_call(
        functools.partial(_attn_prompt_kernel, n_kv=n_kv, q_per_kv=n_q // n_kv, hd=hd,
                          blocks_per_seq=seq // blk, slopes=slopes),
        out_shape=jax.ShapeDtypeStruct((t, qw), BF16),
        grid=(t // blk,),
        in_specs=[
            pl.BlockSpec(memory_space=pltpu.SMEM),
            pl.BlockSpec((blk, qw), lambda i: (i, 0)),
            pl.BlockSpec((blk, kvw), lambda i: (i, 0)),
            pl.BlockSpec((blk, kvw), lambda i: (i, 1)),
            pl.BlockSpec((blk, kvw), lambda i: (prev(i), 0)),
            pl.BlockSpec((blk, kvw), lambda i: (prev(i), 1)),
        ],
        out_specs=pl.BlockSpec((blk, qw), lambda i: (i, 0)),
        compiler_params=pltpu.CompilerParams(dimension_semantics=("parallel",)),
        name="attn_prompt",
    )(sinks, q, kvu, kvu, kvu, kvu)


def _attn_sample_kernel(qbd_ref, ck_ref, cv_ref, kn_ref, vn_ref, bc_ref, bn_ref, sink_ref, o_ref, *,
                        bb, n_kv, hd):
    nt = (((1,), (1,)), ((), ()))
    rows = qbd_ref.shape[1]
    grp = lax.broadcasted_iota(I32, (rows, 1), 0) // (rows // n_kv)
    sink = sink_ref[...]
    for b in range(bb):
        qb = qbd_ref[b]
        s_c = lax.dot_general(qb, ck_ref[b].astype(BF16), nt, preferred_element_type=F32) + bc_ref[...]
        s_n = lax.dot_general(qb, kn_ref[b].astype(BF16), nt, preferred_element_type=F32) + bn_ref[...]
        m = jnp.maximum(jnp.maximum(jnp.max(s_c, axis=-1, keepdims=True),
                                    jnp.max(s_n, axis=-1, keepdims=True)), sink)
        p_c = jnp.exp(s_c - m)
        p_n = jnp.exp(s_n - m)
        denom = (jnp.sum(p_c, axis=-1, keepdims=True) + jnp.sum(p_n, axis=-1, keepdims=True)
                 + jnp.exp(sink - m))
        o = (jnp.dot(p_c.astype(BF16), cv_ref[b].astype(BF16), preferred_element_type=F32)
             + jnp.dot(p_n.astype(BF16), vn_ref[b].astype(BF16), preferred_element_type=F32))
        sel = jnp.zeros((rows, hd), F32)
        for g in range(n_kv):
            sel = sel + jnp.where(grp == g, o[:, g * hd:(g + 1) * hd], 0.0)
        o_ref[b] = sel / denom


def _attn_sample(q_s, k_new, v_new, cache_k, cache_v, sinks, n_kv, hd):
    bd, n, qw = q_s.shape
    n_q = qw // hd
    r = n_q // n_kv
    kvw = n_kv * hd
    w = cache_k.shape[1]
    rows = n_q * n
    npad = 16
    q5 = q_s.reshape(bd, n, n_kv, r, hd)
    qbd = jnp.einsum("bigrd,gh->bgrihd", q5, jnp.eye(n_kv, dtype=q_s.dtype)).reshape(bd, rows, kvw)
    pad = ((0, 0), (0, npad - n), (0, 0))
    k_new = jnp.pad(k_new, pad)
    v_new = jnp.pad(v_new, pad)
    slopes = 2.0 ** (-8.0 * np.arange(1, n_q + 1, dtype=np.float64) / n_q)
    slope_c = np.repeat(slopes, n)
    i_c = np.tile(np.arange(n), n_q)
    dist_c = (w + i_c)[:, None] - np.arange(w)[None, :]
    bias_c = np.where((dist_c >= 0) & (dist_c <= WINDOW), -slope_c[:, None] * dist_c, NEG_BIG)
    dist_n = i_c[:, None] - np.arange(npad)[None, :]
    ok_n = (dist_n >= 0) & (np.arange(npad)[None, :] < n)
    bias_n = np.where(ok_n, -slope_c[:, None] * dist_n, NEG_BIG)
    sink_c = jnp.repeat(sinks.astype(F32), n).reshape(rows, 1)
    bb = 8 if bd % 8 == 0 else 1
    return pl.pallas_call(
        functools.partial(_attn_sample_kernel, bb=bb, n_kv=n_kv, hd=hd),
        out_shape=jax.ShapeDtypeStruct((bd, rows, hd), F32),
        grid=(bd // bb,),
        in_specs=[
            pl.BlockSpec((bb, rows, kvw), lambda i: (i, 0, 0)),
            pl.BlockSpec((bb, w, kvw), lambda i: (i, 0, 0)),
            pl.BlockSpec((bb, w, kvw), lambda i: (i, 0, 0)),
            pl.BlockSpec((bb, npad, kvw), lambda i: (i, 0, 0)),
            pl.BlockSpec((bb, npad, kvw), lambda i: (i, 0, 0)),
            pl.BlockSpec((rows, w), lambda i: (0, 0)),
            pl.BlockSpec((rows, npad), lambda i: (0, 0)),
            pl.BlockSpec((rows, 1), lambda i: (0, 0)),
        ],
        out_specs=pl.BlockSpec((bb, rows, hd), lambda i: (i, 0, 0)),
        compiler_params=pltpu.CompilerParams(dimension_semantics=("parallel",)),
        name="attn_sample",
    )(qbd, cache_k, cache_v, k_new, v_new,
      jnp.asarray(bias_c, F32), jnp.asarray(bias_n, F32), sink_c)


def _pool_prompt_kernel(*refs, tp, gw, tiles_per_seq):
    ng = len(POOL_WINDOWS)
    cur = refs[:ng]
    halo = refs[ng:2 * ng]
    wmix_ref, scale_ref, o_ref, ext_ref = refs[2 * ng:]
    tile_in_seq = pl.program_id(0) % tiles_per_seq
    first = tile_in_seq == 0
    pos = lax.broadcasted_iota(I32, (tp, 1), 0) + tile_in_seq * tp
    for g, w in enumerate(POOL_WINDOWS):
        u = cur[g][...]
        ext_ref[0:HALO, :] = jnp.where(first, 0.0, halo[g][...])
        ext_ref[HALO:HALO + tp, :] = u
        acc = u
        for d in range(1, w):
            acc = acc + ext_ref[HALO - d:HALO - d + tp, :]
        cnt = jnp.minimum(w, pos + 1).astype(F32)
        pooled = acc / cnt - u
        z = jnp.dot(pooled.astype(BF16), wmix_ref[g], preferred_element_type=F32)
        o_ref[:, g * gw:(g + 1) * gw] = (z * scale_ref[:, g * gw:(g + 1) * gw]).astype(o_ref.dtype)


def _pool_prompt(kvu, u_col0, pw, wmix, scale, seq, tp):
    t = kvu.shape[0]
    ng = len(POOL_WINDOWS)
    gw = pw // ng
    tp = min(tp, seq)
    assert seq % tp == 0 and tp % HALO == 0 and u_col0 % gw == 0
    c0 = u_col0 // gw
    hb = tp // HALO
    cur_specs = [pl.BlockSpec((tp, gw), functools.partial(lambda i, g: (i, c0 + g), g=g)) for g in range(ng)]
    halo_specs = [pl.BlockSpec((HALO, gw),
                               functools.partial(lambda i, g: (jnp.maximum(i * hb - 1, 0), c0 + g), g=g))
                  for g in range(ng)]
    return pl.pallas_call(
        functools.partial(_pool_prompt_kernel, tp=tp, gw=gw, tiles_per_seq=seq // tp),
        out_shape=jax.ShapeDtypeStruct((t, pw), BF16),
        grid=(t // tp,),
        in_specs=cur_specs + halo_specs + [
            pl.BlockSpec((ng, gw, gw), lambda i: (0, 0, 0)),
            pl.BlockSpec((1, pw), lambda i: (0, 0)),
        ],
        out_specs=pl.BlockSpec((tp, pw), lambda i: (i, 0)),
        scratch_shapes=[pltpu.VMEM((HALO + tp, gw), F32)],
        compiler_params=pltpu.CompilerParams(dimension_semantics=("parallel",)),
        name="pool_prompt",
    )(*([kvu] * (2 * ng)), wmix, scale)


def _pool_sample_kernel(ext_ref, wmix_ref, scale_ref, o_ref, *, n_new, gw):
    n_prev = ext_ref.shape[0] - n_new
    for i in range(n_new):
        for g, w in enumerate(POOL_WINDOWS):
            cs = slice(g * gw, (g + 1) * gw)
            u = ext_ref[n_prev + i, :, cs]
            acc = u
            for d in range(1, w):
                acc = acc + ext_ref[n_prev + i - d, :, cs]
            cnt = float(min(w, PAST_LEN + i + 1))
            pooled = acc / cnt - u
            z = jnp.dot(pooled.astype(BF16), wmix_ref[g], preferred_element_type=F32)
            o_ref[i, :, cs] = (z * scale_ref[:, cs]).astype(o_ref.dtype)


def _pool_sample(ext_t, n_new, wmix, scale):
    rows, bd, pw = ext_t.shape
    gw = pw // len(POOL_WINDOWS)
    return pl.pallas_call(
        functools.partial(_pool_sample_kernel, n_new=n_new, gw=gw),
        out_shape=jax.ShapeDtypeStruct((n_new, bd, pw), BF16),
        compiler_params=pltpu.CompilerParams(vmem_limit_bytes=_vmem_limit(2 * rows * bd * pw * 4)),
        name="pool_sample",
    )(ext_t, wmix, scale)


def _merge_route_kernel(attn_ref, pool_ref, sga_ref, sgp_ref, x_ref, wa_ref, wp_ref, wo_ref,
                        gain_ref, wr_ref, br_ref, cnt_in_ref,
                        h_ref, tok_ref, idx_ref, rank_ref, gate_ref, cnt_out_ref, cnt_ref):
    i = pl.program_id(0)
    tm = x_ref.shape[0]
    n_e = wr_ref.shape[0]

    @pl.when(i == 0)
    def _():
        cnt_ref[...] = cnt_in_ref[...]

    a = jnp.dot(attn_ref[...], wa_ref[...], preferred_element_type=F32)
    p = jnp.dot(pool_ref[...], wp_ref[...], preferred_element_type=F32)
    mixed = sga_ref[...].astype(F32) * a + sgp_ref[...].astype(F32) * p
    h = x_ref[...] + jnp.dot(mixed.astype(BF16), wo_ref[...], preferred_element_type=F32)
    h_ref[...] = h
    tok = _rms(h, gain_ref[...])
    tok_ref[...] = tok

    logits = lax.dot_general(wr_ref[...], tok.astype(BF16), (((1,), (1,)), ((), ())),
                             preferred_element_type=F32) + br_ref[...]
    eid = lax.broadcasted_iota(I32, (n_e, tm), 0).astype(F32)
    work = logits
    member = jnp.zeros((n_e, tm), F32)
    vals, idxs = [], []
    for _ in range(TOP_K):
        mk = jnp.max(work, axis=0, keepdims=True)
        ik = jnp.min(jnp.where(work == mk, eid, float(n_e)), axis=0, keepdims=True)
        sel = eid == ik
        vals.append(mk)
        idxs.append(ik)
        member = member + sel.astype(F32)
        work = jnp.where(sel, -jnp.inf, work)
    ex = [jnp.exp(v - vals[0]) for v in vals]
    den = ex[0]
    for e in ex[1:]:
        den = den + e
    rr = lax.broadcasted_iota(I32, (tm, tm), 0)
    cc = lax.broadcasted_iota(I32, (tm, tm), 1)
    earlier = (rr < cc).astype(BF16)
    before = jnp.dot(member.astype(BF16), earlier, preferred_element_type=F32) + cnt_ref[...]
    ranks = [jnp.sum(jnp.where(eid == ik, before, 0.0), axis=0, keepdims=True) for ik in idxs]
    cnt_ref[...] = cnt_ref[...] + jnp.sum(member, axis=1, keepdims=True)

    krow = lax.broadcasted_iota(I32, (TOP_K, tm), 0)

    def rows(parts):
        out = jnp.zeros((TOP_K, tm), F32)
        for k, c in enumerate(parts):
            out = jnp.where(krow == k, c, out)
        return out

    idx_ref[...] = rows(idxs).astype(I32)
    rank_ref[...] = rows(ranks).astype(I32)
    gate_ref[...] = rows([e / den for e in ex])
    cnt_out_ref[...] = cnt_ref[...]


def _merge_route(attn, pool_o, sg, x, wa, wp, wo, gain, wr, br, cnt_in, tm):
    t, d = x.shape
    qw, pw, n_e = attn.shape[1], pool_o.shape[1], wr.shape[0]
    tm = min(tm, t)
    assert t % tm == 0
    per_tok = pl.BlockSpec((TOP_K, tm), lambda i: (0, i))
    row = lambda w: pl.BlockSpec((tm, w), lambda i: (i, 0))
    whole = lambda a: pl.BlockSpec(a.shape, lambda i: (0,) * a.ndim, pipeline_mode=pl.Buffered(1))
    out_shape = [
        jax.ShapeDtypeStruct((t, d), F32),
        jax.ShapeDtypeStruct((t, d), F32),
        jax.ShapeDtypeStruct((TOP_K, t), I32),
        jax.ShapeDtypeStruct((TOP_K, t), I32),
        jax.ShapeDtypeStruct((TOP_K, t), F32),
        jax.ShapeDtypeStruct((n_e, 1), F32),
    ]
    in_specs = [row(qw), row(pw), pl.BlockSpec((tm, d), lambda i: (i, 0)),
                pl.BlockSpec((tm, d), lambda i: (i, 1)), row(d),
                whole(wa), whole(wp), whole(wo), whole(gain), whole(wr), whole(br), whole(cnt_in)]
    est = (wa.size + wp.size + wo.size) * 2 + 2 * tm * (qw + pw + 2 * d) * 2 + 6 * tm * d * 4 + 8 * tm * d * 4
    return pl.pallas_call(
        _merge_route_kernel,
        out_shape=out_shape,
        grid=(t // tm,),
        in_specs=in_specs,
        out_specs=[row(d), row(d), per_tok, per_tok, per_tok,
                   pl.BlockSpec((n_e, 1), lambda i: (0, 0))],
        scratch_shapes=[pltpu.VMEM((n_e, 1), F32)],
        compiler_params=pltpu.CompilerParams(
            dimension_semantics=("arbitrary",), vmem_limit_bytes=_vmem_limit(est)),
        name="merge_route",
    )(attn, pool_o, sg, sg, x, wa, wp, wo, gain, wr, br, cnt_in)


def _dispatch_kernel(dest_a_ref, dest_b_ref, pad_off_ref, pad_len_ref, meta_ref,
                     tok_a_ref, tok_b_ref, xs_ref, zero_ref, sem, pad_sem, tail_sem, *,
                     tm, t_a, t_b, n_e):
    i = pl.program_id(0)
    n_a = t_a // tm
    p = zero_ref.shape[0]
    n_tiles = xs_ref.shape[0] // p

    def pad_copy(d):
        return pltpu.make_async_copy(zero_ref.at[pl.ds(0, 1)], xs_ref.at[pl.ds(d, 1)], pad_sem)

    def tail_copy(b):
        return pltpu.make_async_copy(zero_ref, xs_ref.at[pl.ds(pl.multiple_of(b * p, p), p)], tail_sem)

    @pl.when(i == 0)
    def _():
        zero_ref[...] = jnp.zeros_like(zero_ref)

        def pad_group(e, c):
            off = pad_off_ref[e]

            def one(r, c2):
                pad_copy(off + r).start()
                return c2

            lax.fori_loop(0, pad_len_ref[e], one, 0)
            return c

        lax.fori_loop(0, n_e, pad_group, 0)

        def tail(b, c):
            tail_copy(b).start()
            return c

        lax.fori_loop(meta_ref[0], n_tiles, tail, 0)

    def scatter(tok_ref, dest_ref, t, base):
        def row_copy(r, d):
            return pltpu.make_async_copy(tok_ref.at[pl.ds(r, 1)], xs_ref.at[pl.ds(d, 1)], sem)

        def issue(r, c):
            for k in range(TOP_K):
                row_copy(r, dest_ref[k * t + base + r]).start()
            return c

        lax.fori_loop(0, tm, issue, 0)

        def drain(r, c):
            for k in range(TOP_K):
                row_copy(r, 0).wait()
            return c

        lax.fori_loop(0, tm, drain, 0)

    @pl.when(i < n_a)
    def _():
        scatter(tok_a_ref, dest_a_ref, t_a, i * tm)

    @pl.when(i >= n_a)
    def _():
        scatter(tok_b_ref, dest_b_ref, t_b, (i - n_a) * tm)

    @pl.when(i == 0)
    def _():
        def pad_wait(r, c):
            pad_copy(0).wait()
            return c

        lax.fori_loop(0, meta_ref[1], pad_wait, 0)

        def tail_wait(b, c):
            tail_copy(b).wait()
            return c

        lax.fori_loop(meta_ref[0], n_tiles, tail_wait, 0)


def _dispatch(dest_a, dest_b, pad_off, pad_len, meta, tok_a, tok_b, n_slots, tm):
    (t_a, d), t_b = tok_a.shape, tok_b.shape[0]
    assert t_a % tm == 0 and t_b % tm == 0 and n_slots % EXPERT_ROWS == 0
    n_a, n_b = t_a // tm, t_b // tm
    n_e = pad_off.shape[0]
    return pl.pallas_call(
        functools.partial(_dispatch_kernel, tm=tm, t_a=t_a, t_b=t_b, n_e=n_e),
        out_shape=jax.ShapeDtypeStruct((n_slots, d), F32),
        grid_spec=pltpu.PrefetchScalarGridSpec(
            num_scalar_prefetch=5,
            grid=(n_a + n_b,),
            in_specs=[pl.BlockSpec((tm, d), lambda i, *_: (jnp.minimum(i, n_a - 1), 0)),
                      pl.BlockSpec((tm, d), lambda i, *_: (jnp.maximum(i - n_a, 0), 0))],
            out_specs=pl.BlockSpec(memory_space=pl.ANY),
            scratch_shapes=[pltpu.VMEM((EXPERT_ROWS, d), F32), pltpu.SemaphoreType.DMA(()),
                            pltpu.SemaphoreType.DMA(()), pltpu.SemaphoreType.DMA(())],
        ),
        compiler_params=pltpu.CompilerParams(
            dimension_semantics=("arbitrary",), has_side_effects=True),
        name="dispatch",
    )(dest_a, dest_b, pad_off, pad_len, meta, tok_a, tok_b)


def _is_new_expert(be_ref, i):
    return jnp.logical_or(i == 0, be_ref[i] != be_ref[jnp.maximum(i - 1, 0)])


def _expert_up_kernel(be_ref, ok_ref, xs_ref, wg_ref, wu_ref, bg_ref, bu_ref, h_ref, wg_bf, wu_bf):
    i = pl.program_id(1)

    @pl.when(_is_new_expert(be_ref, i))
    def _():
        wg_bf[...] = wg_ref[0].astype(BF16)
        wu_bf[...] = wu_ref[0].astype(BF16)

    @pl.when(ok_ref[i] == 1)
    def _():
        x = xs_ref[...].astype(BF16)
        tf = h_ref.shape[1]
        cw = min(tf, 256)
        for c in range(tf // cw):
            cs = slice(c * cw, (c + 1) * cw)
            glu = jnp.dot(x, wg_bf[:, cs], preferred_element_type=F32) + bg_ref[0, :, cs]
            lin = jnp.dot(x, wu_bf[:, cs], preferred_element_type=F32) + bu_ref[0, :, cs]
            glu = jnp.minimum(glu, SWIGLU_LIMIT)
            lin = jnp.clip(lin, -SWIGLU_LIMIT, SWIGLU_LIMIT)
            h_ref[:, cs] = (glu * jax.nn.sigmoid(SWIGLU_ALPHA * glu) * (lin + 1.0)).astype(h_ref.dtype)

    @pl.when(ok_ref[i] == 0)
    def _():
        h_ref[...] = jnp.zeros_like(h_ref)


def _expert_up(blk_e, blk_ok, xs, wg, wu, bg, bu, tf):
    ns, d = xs.shape
    n_e, _, dff = wg.shape
    p = EXPERT_ROWS
    tf = min(tf, dff)
    assert ns % p == 0 and dff % tf == 0
    est = 2 * 2 * d * tf * 4 + 2 * d * tf * 2 + 2 * p * d * 4 + 2 * p * tf * 2 + 4 * p * tf * 4
    wspec = pl.BlockSpec((1, d, tf), lambda j, i, be, ok: (be[i], 0, j))
    bspec = pl.BlockSpec((1, 1, tf), lambda j, i, be, ok: (be[i], 0, j))
    return pl.pallas_call(
        _expert_up_kernel,
        out_shape=jax.ShapeDtypeStruct((ns, dff), BF16),
        grid_spec=pltpu.PrefetchScalarGridSpec(
            num_scalar_prefetch=2,
            grid=(dff // tf, ns // p),
            in_specs=[pl.BlockSpec((p, d), lambda j, i, be, ok: (i, 0)), wspec, wspec, bspec, bspec],
            out_specs=pl.BlockSpec((p, tf), lambda j, i, be, ok: (i, j)),
            scratch_shapes=[pltpu.VMEM((d, tf), BF16), pltpu.VMEM((d, tf), BF16)],
        ),
        compiler_params=pltpu.CompilerParams(
            dimension_semantics=("arbitrary", "arbitrary"), vmem_limit_bytes=_vmem_limit(est)),
        name="expert_up",
    )(blk_e, blk_ok, xs, wg, wu, bg.reshape(n_e, 1, dff), bu.reshape(n_e, 1, dff))


def _expert_down_kernel(be_ref, ok_ref, h_ref, wd_ref, bd_ref, y_ref, wd_bf):
    i = pl.program_id(1)

    @pl.when(_is_new_expert(be_ref, i))
    def _():
        wd_bf[...] = wd_ref[0].astype(BF16)

    @pl.when(ok_ref[i] == 1)
    def _():
        y_ref[...] = jnp.dot(h_ref[...], wd_bf[...], preferred_element_type=F32) + bd_ref[0]

    @pl.when(ok_ref[i] == 0)
    def _():
        y_ref[...] = jnp.zeros_like(y_ref)


def _expert_down(blk_e, blk_ok, hmid, wd, bd, tn):
    ns, dff = hmid.shape
    n_e, _, d = wd.shape
    p = EXPERT_ROWS
    tn = min(tn, d)
    assert ns % p == 0 and d % tn == 0
    est = 2 * dff * tn * 4 + dff * tn * 2 + 2 * p * dff * 2 + 2 * p * tn * 4 + 2 * p * tn * 4
    return pl.pallas_call(
        _expert_down_kernel,
        out_shape=jax.ShapeDtypeStruct((ns, d), F32),
        grid_spec=pltpu.PrefetchScalarGridSpec(
            num_scalar_prefetch=2,
            grid=(d // tn, ns // p),
            in_specs=[pl.BlockSpec((p, dff), lambda j, i, be, ok: (i, 0)),
                      pl.BlockSpec((1, dff, tn), lambda j, i, be, ok: (be[i], 0, j)),
                      pl.BlockSpec((1, 1, tn), lambda j, i, be, ok: (be[i], 0, j))],
            out_specs=pl.BlockSpec((p, tn), lambda j, i, be, ok: (i, j)),
            scratch_shapes=[pltpu.VMEM((dff, tn), BF16)],
        ),
        compiler_params=pltpu.CompilerParams(
            dimension_semantics=("arbitrary", "arbitrary"), vmem_limit_bytes=_vmem_limit(est)),
        name="expert_down",
    )(blk_e, blk_ok, hmid, wd, bd.reshape(n_e, 1, d))


def _combine_kernel(dest_ref, h_ref, gate_ref, gain_ref, ys_ref, o_ref, buf_ref, sem, *, tm, t):
    base = pl.program_id(0) * tm

    def row_copy(r, k, d):
        return pltpu.make_async_copy(ys_ref.at[pl.ds(d, 1)], buf_ref.at[k, pl.ds(r, 1)], sem)

    def issue(r, c):
        for k in range(TOP_K):
            row_copy(r, k, dest_ref[k * t + base + r]).start()
        return c

    lax.fori_loop(0, tm, issue, 0)

    def drain(r, c):
        for k in range(TOP_K):
            row_copy(r, k, 0).wait()
        return c

    lax.fori_loop(0, tm, drain, 0)

    acc = h_ref[...]
    gates = gate_ref[...]
    for k in range(TOP_K):
        acc = acc + gates[:, k:k + 1] * buf_ref[k]
    o_ref[...] = _rms(acc, gain_ref[...])


def _combine(dest_flat, h, gates, gain, ys, tm):
    t, d = h.shape
    tm = min(tm, t)
    assert t % tm == 0
    est = TOP_K * tm * d * 4 + 4 * tm * d * 4 + 2 * tm * d * 4
    return pl.pallas_call(
        functools.partial(_combine_kernel, tm=tm, t=t),
        out_shape=jax.ShapeDtypeStruct((t, d), F32),
        grid_spec=pltpu.PrefetchScalarGridSpec(
            num_scalar_prefetch=1,
            grid=(t // tm,),
            in_specs=[pl.BlockSpec((tm, d), lambda i, dest: (i, 0)),
                      pl.BlockSpec((tm, TOP_K), lambda i, dest: (i, 0)),
                      pl.BlockSpec((1, d), lambda i, dest: (0, 0)),
                      pl.BlockSpec(memory_space=pl.ANY)],
            out_specs=pl.BlockSpec((tm, d), lambda i, dest: (i, 0)),
            scratch_shapes=[pltpu.VMEM((TOP_K, tm, d), F32), pltpu.SemaphoreType.DMA(())],
        ),
        compiler_params=pltpu.CompilerParams(
            dimension_semantics=("arbitrary",), vmem_limit_bytes=_vmem_limit(est)),
        name="combine",
    )(dest_flat, h, gates, gain, ys)


def kernel(x_prompt, x_sample, cache_k, cache_v, state_pool, norm_mix, w_in, attn_sinks, w_pool_mix,
           pool_scale, w_attn_out, w_pool_out, w_out, norm_ffn, w_router, b_router, w_gate, b_gate,
           w_up, b_up, w_down, b_down, norm_final):
    assert norm_mix.shape[0] == 1, "single-layer step"
    bp, sp, d = x_prompt.shape
    bs, ns, _ = x_sample.shape
    n_kv, hd = cache_k.shape[-2:]
    n_q = attn_sinks.shape[1]
    qw, kvw = n_q * hd, n_kv * hd
    pw = state_pool.shape[-1]
    n_e = w_router.shape[-1]
    tp_, ts_ = bp * sp, bs * ns
    t_all = tp_ + ts_

    gain_mix = norm_mix[0].reshape(1, d)
    w_in_bf = w_in[0].astype(BF16)
    w_q = w_in_bf[:, :qw]
    w_kvu = w_in_bf[:, qw:qw + 2 * kvw + pw]
    w_g = w_in_bf[:, qw + 2 * kvw + pw:]
    wmix = w_pool_mix[0].astype(BF16)
    pscale = pool_scale[0].reshape(1, pw)
    wa = w_attn_out[0].astype(BF16)
    wp = w_pool_out[0].astype(BF16)
    wo = w_out[0].astype(BF16)
    sinks = attn_sinks[0].astype(F32)

    def project(x2d):
        xn = _rms_cast(x2d, gain_mix, 512)
        q = _proj(xn, w_q, BF16, "scale", 1024, 1024, "proj_q")
        kvu = _proj(xn, w_kvu, F32, "none", 1024, 768, "proj_kvu")
        sg = _proj(xn, w_g, BF16, "sigmoid", 1024, 1024, "proj_gates")
        return q, kvu, sg

    xp = x_prompt.reshape(tp_, d)
    q_p, kvu_p, sg_p = project(xp)
    attn_p = _attn_prompt(q_p, kvu_p, sinks, sp, n_kv, hd)
    pool_p = _pool_prompt(kvu_p, 2 * kvw, pw, wmix, pscale, sp, 512)

    xs_ = x_sample.reshape(ts_, d)
    q_s, kvu_s, sg_s = project(xs_)
    k_s = kvu_s[:, :kvw].reshape(bs, ns, kvw)
    v_s = kvu_s[:, kvw:2 * kvw].reshape(bs, ns, kvw)
    u_s = kvu_s[:, 2 * kvw:].reshape(bs, ns, pw)
    ck = cache_k[0].reshape(bs, -1, kvw)
    cv = cache_v[0].reshape(bs, -1, kvw)
    o_s = _attn_sample(q_s.reshape(bs, ns, qw), k_s, v_s, ck, cv, sinks, n_kv, hd)
    attn_s = (o_s.reshape(bs, n_kv, n_q // n_kv, ns, hd).transpose(0, 3, 1, 2, 4)
              .reshape(ts_, qw).astype(BF16))
    ext = jnp.concatenate([state_pool[0], u_s], axis=1)
    pool_s = _pool_sample(ext.transpose(1, 0, 2), ns, wmix, pscale)
    pool_s = pool_s.transpose(1, 0, 2).reshape(ts_, pw)

    gain_ffn = norm_ffn[0].reshape(1, d)
    wr = w_router[0].T.astype(BF16)
    br = b_router[0].reshape(n_e, 1).astype(F32)
    tm = 256
    h_p, tok_p, idx_p, rank_p, gates_p, cnt_p = _merge_route(
        attn_p, pool_p, sg_p, xp, wa, wp, wo, gain_ffn, wr, br, jnp.zeros((n_e, 1), F32), tm)
    h_s, tok_s, idx_s, rank_s, gates_s, counts = _merge_route(
        attn_s, pool_s, sg_s, xs_, wa, wp, wo, gain_ffn, wr, br, cnt_p, tm)

    p = EXPERT_ROWS
    cnt = counts[:, 0].astype(I32)
    padded = (cnt + p - 1) // p * p
    pad_end = jnp.cumsum(padded)
    pad_start = pad_end - padded
    experts = jnp.arange(n_e, dtype=I32)[:, None, None]

    def slot_ids(idx, rank):
        start = jnp.sum(jnp.where(idx[None] == experts, pad_start[:, None, None], 0), axis=0)
        return (start + rank).reshape(-1)

    dest_p = slot_ids(idx_p, rank_p)
    dest_s = slot_ids(idx_s, rank_s)
    n_slots = -(-(t_all * TOP_K) // p) * p + n_e * p
    n_tiles = n_slots // p
    tile_start = jnp.arange(n_tiles, dtype=I32) * p
    blk_ok = (tile_start < pad_end[-1]).astype(I32)
    blk_e = jnp.sum((pad_end[None, :] <= tile_start[:, None]).astype(I32), axis=1)
    blk_e = jnp.minimum(blk_e, n_e - 1)
    blk_e = jnp.where(blk_ok == 1, blk_e, jnp.max(blk_e * blk_ok))

    meta = jnp.stack([pad_end[-1] // p, jnp.sum(padded - cnt)]).astype(I32)
    xs_sorted = _dispatch(dest_p, dest_s, pad_start + cnt, padded - cnt, meta, tok_p, tok_s, n_slots, tm)
    hmid = _expert_up(blk_e, blk_ok, xs_sorted, w_gate[0], w_up[0], b_gate[0], b_up[0], 1024)
    ys = _expert_down(blk_e, blk_ok, hmid, w_down[0], b_down[0], 2048)

    gain_fin = norm_final.reshape(1, d)
    y_p = _combine(dest_p, h_p, gates_p.T, gain_fin, ys, tm)
    y_s = _combine(dest_s, h_s, gates_s.T, gain_fin, ys, tm)

    keep = min(WINDOW, sp)
    tail_p = kvu_p.reshape(bp, sp, 2 * kvw + pw)[:, sp - keep:]
    n_rows = cache_k.shape[2]
    new_k_s = jnp.concatenate([cache_k[0], k_s.reshape(bs, ns, n_kv, hd)], axis=1)[:, -n_rows:]
    new_v_s = jnp.concatenate([cache_v[0], v_s.reshape(bs, ns, n_kv, hd)], axis=1)[:, -n_rows:]
    return (y_p.reshape(bp, sp, d), y_s.reshape(bs, ns, d),
            tail_p[:, :, :kvw].reshape(1, bp, keep, n_kv, hd),
            tail_p[:, :, kvw:2 * kvw].reshape(1, bp, keep, n_kv, hd),
            tail_p[None, :, keep - POOL_STATE:, 2 * kvw:],
            new_k_s[None], new_v_s[None], ext[None, :, -POOL_STATE:])
```

```python
import functools

import jax
import jax.numpy as jnp
import numpy as np
from jax import lax
from jax.experimental import pallas as pl
from jax.experimental.pallas import tpu as pltpu

F32 = jnp.float32
BF16 = jnp.bfloat16
I32 = jnp.int32

WINDOW = 128
POOL_WINDOWS = (2, 4, 8, 16)
POOL_STATE = max(POOL_WINDOWS) - 1
PAST_LEN = 16384
TOP_K = 4
SWIGLU_LIMIT = 7.0
SWIGLU_ALPHA = 1.702
RMS_EPS = 1e-5
NEG_BIG = -1e30

V7X_VMEM_BYTES = 64 * 1024 * 1024
EXPERT_ROWS = 512
EXPERT_CHUNK = 256
HALO = 16


def _vmem_limit(nbytes):
    return int(min(nbytes + (8 << 20), V7X_VMEM_BYTES - (4 << 20)))


def _rms(x, gain):
    ms = jnp.mean(x * x, axis=-1, keepdims=True)
    return x * lax.rsqrt(ms + RMS_EPS) * gain


def _rms_cast_kernel(x_ref, g_ref, o_ref):
    o_ref[...] = _rms(x_ref[...], g_ref[...]).astype(o_ref.dtype)


def _rms_cast(x, gain, tm):
    t, d = x.shape
    tm = min(tm, t)
    assert t % tm == 0
    return pl.pallas_call(
        _rms_cast_kernel,
        out_shape=jax.ShapeDtypeStruct((t, d), BF16),
        grid=(t // tm,),
        in_specs=[pl.BlockSpec((tm, d), lambda i: (i, 0)), pl.BlockSpec((1, d), lambda i: (0, 0))],
        out_specs=pl.BlockSpec((tm, d), lambda i: (i, 0)),
        compiler_params=pltpu.CompilerParams(
            dimension_semantics=("parallel",), vmem_limit_bytes=_vmem_limit(16 * tm * d)),
        name="rms_cast",
    )(x, gain)


def _proj_kernel(xn_ref, w_ref, o_ref, *, epilogue):
    acc = jnp.dot(xn_ref[...], w_ref[...], preferred_element_type=F32)
    if epilogue == "sigmoid":
        acc = jax.nn.sigmoid(acc)
    elif epilogue == "scale":
        acc = acc * 0.125
    o_ref[...] = acc.astype(o_ref.dtype)


def _proj(xn, w, out_dtype, epilogue, tm, tn, name):
    t, d = xn.shape
    n = w.shape[1]
    tm = min(tm, t)
    tn = min(tn, n)
    assert t % tm == 0 and n % tn == 0
    est = 2 * tm * d * 2 + 2 * d * tn * 2 + 2 * tm * tn * 4 + 2 * tm * tn * 4
    return pl.pallas_call(
        functools.partial(_proj_kernel, epilogue=epilogue),
        out_shape=jax.ShapeDtypeStruct((t, n), out_dtype),
        grid=(n // tn, t // tm),
        in_specs=[
            pl.BlockSpec((tm, d), lambda j, i: (i, 0)),
            pl.BlockSpec((d, tn), lambda j, i: (0, j)),
        ],
        out_specs=pl.BlockSpec((tm, tn), lambda j, i: (i, j)),
        compiler_params=pltpu.CompilerParams(
            dimension_semantics=("parallel", "parallel"),
            vmem_limit_bytes=_vmem_limit(est)),
        name=name,
    )(xn, w)


def _attn_prompt_kernel(sink_ref, q_ref, kc_ref, vc_ref, kp_ref, vp_ref, o_ref, *,
                        n_kv, q_per_kv, hd, blocks_per_seq, slopes):
    blk = WINDOW
    rq = q_per_kv
    nt = (((1,), (1,)), ((), ()))
    has_prev = (pl.program_id(0) % blocks_per_seq) > 0
    qi = lax.broadcasted_iota(I32, (blk, 2 * blk), 0)
    kj = lax.broadcasted_iota(I32, (blk, 2 * blk), 1)
    dist = qi + blk - kj
    valid = (dist >= 0) & (dist <= WINDOW) & ((kj >= blk) | has_prev)
    distf = dist.astype(F32)

    def scores(g):
        cs = slice(g * hd, (g + 1) * hd)
        k = jnp.concatenate([kp_ref[:, cs], kc_ref[:, cs]], axis=0).astype(BF16)
        qs = jnp.concatenate([q_ref[:, (g * rq + r) * hd:(g * rq + r + 1) * hd] for r in range(rq)], axis=0)
        s = lax.dot_general(qs, k, nt, preferred_element_type=F32)
        slabs, maxes = [], []
        for r in range(rq):
            h = g * rq + r
            sr = jnp.where(valid, s[r * blk:(r + 1) * blk] - slopes[h] * distf, NEG_BIG)
            slabs.append(sr)
            maxes.append(jnp.maximum(jnp.max(sr, axis=-1, keepdims=True), sink_ref[h]))
        return slabs, maxes

    def finish(g, slabs, maxes):
        cs = slice(g * hd, (g + 1) * hd)
        v = jnp.concatenate([vp_ref[:, cs], vc_ref[:, cs]], axis=0).astype(BF16)
        probs, inv = [], []
        for r in range(rq):
            p = jnp.exp(slabs[r] - maxes[r])
            denom = jnp.sum(p, axis=-1, keepdims=True) + jnp.exp(sink_ref[g * rq + r] - maxes[r])
            probs.append(p.astype(BF16))
            inv.append(1.0 / denom)
        o = jnp.dot(jnp.concatenate(probs, axis=0), v, preferred_element_type=F32)
        for r in range(rq):
            h = g * rq + r
            o_ref[:, h * hd:(h + 1) * hd] = (o[r * blk:(r + 1) * blk] * inv[r]).astype(o_ref.dtype)

    pending = scores(0)
    for g in range(n_kv):
        nxt = scores(g + 1) if g + 1 < n_kv else None
        finish(g, *pending)
        pending = nxt


def _attn_prompt(q, kvu, sinks, seq, n_kv, hd):
    t, qw = q.shape
    n_q = qw // hd
    kvw = n_kv * hd
    blk = WINDOW
    assert t % blk == 0 and seq % blk == 0
    slopes = tuple(float(2.0 ** (-8.0 * (h + 1) / n_q)) for h in range(n_q))
    prev = lambda i: jnp.maximum(i - 1, 0)
    return pl.pallas_call(
        functools.partial(_attn_prompt_kernel, n_kv=n_kv, q_per_kv=n_q // n_kv, hd=hd,
                          blocks_per_seq=seq // blk, slopes=slopes),
        out_shape=jax.ShapeDtypeStruct((t, qw), BF16),
        grid=(t // blk,),
        in_specs=[
            pl.BlockSpec(memory_space=pltpu.SMEM),
            pl.BlockSpec((blk, qw), lambda i: (i, 0)),
            pl.BlockSpec((blk, kvw), lambda i: (i, 0)),
            pl.BlockSpec((blk, kvw), lambda i: (i, 1)),
            pl.BlockSpec((blk, kvw), lambda i: (prev(i), 0)),
            pl.BlockSpec((blk, kvw), lambda i: (prev(i), 1)),
        ],
        out_specs=pl.BlockSpec((blk, qw), lambda i: (i, 0)),
        compiler_params=pltpu.CompilerParams(dimension_semantics=("parallel",)),
        name="attn_prompt",
    )(sinks, q, kvu, kvu, kvu, kvu)


def _attn_sample_kernel(qbd_ref, ck_ref, cv_ref, kn_ref, vn_ref, bc_ref, bn_ref, sink_ref, o_ref, *,
                        bb, n_kv, hd):
    nt = (((1,), (1,)), ((), ()))
    rows = qbd_ref.shape[1]
    grp = lax.broadcasted_iota(I32, (rows, 1), 0) // (rows // n_kv)
    sink = sink_ref[...]
    for b in range(bb):
        qb = qbd_ref[b]
        s_c = lax.dot_general(qb, ck_ref[b].astype(BF16), nt, preferred_element_type=F32) + bc_ref[...]
        s_n = lax.dot_general(qb, kn_ref[b].astype(BF16), nt, preferred_element_type=F32) + bn_ref[...]
        m = jnp.maximum(jnp.maximum(jnp.max(s_c, axis=-1, keepdims=True),
                                    jnp.max(s_n, axis=-1, keepdims=True)), sink)
        p_c = jnp.exp(s_c - m)
        p_n = jnp.exp(s_n - m)
        denom = (jnp.sum(p_c, axis=-1, keepdims=True) + jnp.sum(p_n, axis=-1, keepdims=True)
                 + jnp.exp(sink - m))
        o = (jnp.dot(p_c.astype(BF16), cv_ref[b].astype(BF16), preferred_element_type=F32)
             + jnp.dot(p_n.astype(BF16), vn_ref[b].astype(BF16), preferred_element_type=F32))
        sel = jnp.zeros((rows, hd), F32)
        for g in range(n_kv):
            sel = sel + jnp.where(grp == g, o[:, g * hd:(g + 1) * hd], 0.0)
        o_ref[b] = sel / denom


def _attn_sample(q_s, k_new, v_new, cache_k, cache_v, sinks, n_kv, hd):
    bd, n, qw = q_s.shape
    n_q = qw // hd
    r = n_q // n_kv
    kvw = n_kv * hd
    w = cache_k.shape[1]
    rows = n_q * n
    npad = 16
    q5 = q_s.reshape(bd, n, n_kv, r, hd)
    qbd = jnp.einsum("bigrd,gh->bgrihd", q5, jnp.eye(n_kv, dtype=q_s.dtype)).reshape(bd, rows, kvw)
    pad = ((0, 0), (0, npad - n), (0, 0))
    k_new = jnp.pad(k_new, pad)
    v_new = jnp.pad(v_new, pad)
    slopes = 2.0 ** (-8.0 * np.arange(1, n_q + 1, dtype=np.float64) / n_q)
    slope_c = np.repeat(slopes, n)
    i_c = np.tile(np.arange(n), n_q)
    dist_c = (w + i_c)[:, None] - np.arange(w)[None, :]
    bias_c = np.where((dist_c >= 0) & (dist_c <= WINDOW), -slope_c[:, None] * dist_c, NEG_BIG)
    dist_n = i_c[:, None] - np.arange(npad)[None, :]
    ok_n = (dist_n >= 0) & (np.arange(npad)[None, :] < n)
    bias_n = np.where(ok_n, -slope_c[:, None] * dist_n, NEG_BIG)
    sink_c = jnp.repeat(sinks.astype(F32), n).reshape(rows, 1)
    bb = 8 if bd % 8 == 0 else 1
    return pl.pallas_call(
        functools.partial(_attn_sample_kernel, bb=bb, n_kv=n_kv, hd=hd),
        out_shape=jax.ShapeDtypeStruct((bd, rows, hd), F32),
        grid=(bd // bb,),
        in_specs=[
            pl.BlockSpec((bb, rows, kvw), lambda i: (i, 0, 0)),
            pl.BlockSpec((bb, w, kvw), lambda i: (i, 0, 0)),
            pl.BlockSpec((bb, w, kvw), lambda i: (i, 0, 0)),
            pl.BlockSpec((bb, npad, kvw), lambda i: (i, 0, 0)),
            pl.BlockSpec((bb, npad, kvw), lambda i: (i, 0, 0)),
            pl.BlockSpec((rows, w), lambda i: (0, 0)),
            pl.BlockSpec((rows, npad), lambda i: (0, 0)),
            pl.BlockSpec((rows, 1), lambda i: (0, 0)),
        ],
        out_specs=pl.BlockSpec((bb, rows, hd), lambda i: (i, 0, 0)),
        compiler_params=pltpu.CompilerParams(dimension_semantics=("parallel",)),
        name="attn_sample",
    )(qbd, cache_k, cache_v, k_new, v_new,
      jnp.asarray(bias_c, F32), jnp.asarray(bias_n, F32), sink_c)


def _pool_prompt_kernel(*refs, tp, gw, tiles_per_seq):
    ng = len(POOL_WINDOWS)
    cur = refs[:ng]
    halo = refs[ng:2 * ng]
    wmix_ref, scale_ref, o_ref, ext_ref = refs[2 * ng:]
    tile_in_seq = pl.program_id(0) % tiles_per_seq
    first = tile_in_seq == 0
    pos = lax.broadcasted_iota(I32, (tp, 1), 0) + tile_in_seq * tp
    for g, w in enumerate(POOL_WINDOWS):
        u = cur[g][...]
        ext_ref[0:HALO, :] = jnp.where(first, 0.0, halo[g][...])
        ext_ref[HALO:HALO + tp, :] = u
        acc = u
        for d in range(1, w):
            acc = acc + ext_ref[HALO - d:HALO - d + tp, :]
        cnt = jnp.minimum(w, pos + 1).astype(F32)
        pooled = acc / cnt - u
        z = jnp.dot(pooled.astype(BF16), wmix_ref[g], preferred_element_type=F32)
        o_ref[:, g * gw:(g + 1) * gw] = (z * scale_ref[:, g * gw:(g + 1) * gw]).astype(o_ref.dtype)


def _pool_prompt(kvu, u_col0, pw, wmix, scale, seq, tp):
    t = kvu.shape[0]
    ng = len(POOL_WINDOWS)
    gw = pw // ng
    tp = min(tp, seq)
    assert seq % tp == 0 and tp % HALO == 0 and u_col0 % gw == 0
    c0 = u_col0 // gw
    hb = tp // HALO
    cur_specs = [pl.BlockSpec((tp, gw), functools.partial(lambda i, g: (i, c0 + g), g=g)) for g in range(ng)]
    halo_specs = [pl.BlockSpec((HALO, gw),
                               functools.partial(lambda i, g: (jnp.maximum(i * hb - 1, 0), c0 + g), g=g))
                  for g in range(ng)]
    return pl.pallas_call(
        functools.partial(_pool_prompt_kernel, tp=tp, gw=gw, tiles_per_seq=seq // tp),
        out_shape=jax.ShapeDtypeStruct((t, pw), BF16),
        grid=(t // tp,),
        in_specs=cur_specs + halo_specs + [
            pl.BlockSpec((ng, gw, gw), lambda i: (0, 0, 0)),
            pl.BlockSpec((1, pw), lambda i: (0, 0)),
        ],
        out_specs=pl.BlockSpec((tp, pw), lambda i: (i, 0)),
        scratch_shapes=[pltpu.VMEM((HALO + tp, gw), F32)],
        compiler_params=pltpu.CompilerParams(dimension_semantics=("parallel",)),
        name="pool_prompt",
    )(*([kvu] * (2 * ng)), wmix, scale)


def _pool_sample_kernel(ext_ref, wmix_ref, scale_ref, o_ref, *, n_new, gw):
    n_prev = ext_ref.shape[0] - n_new
    for i in range(n_new):
        for g, w in enumerate(POOL_WINDOWS):
            cs = slice(g * gw, (g + 1) * gw)
            u = ext_ref[n_prev + i, :, cs]
            acc = u
            for d in range(1, w):
                acc = acc + ext_ref[n_prev + i - d, :, cs]
            cnt = float(min(w, PAST_LEN + i + 1))
            pooled = acc / cnt - u
            z = jnp.dot(pooled.astype(BF16), wmix_ref[g], preferred_element_type=F32)
            o_ref[i, :, cs] = (z * scale_ref[:, cs]).astype(o_ref.dtype)


def _pool_sample(ext_t, n_new, wmix, scale):
    rows, bd, pw = ext_t.shape
    gw = pw // len(POOL_WINDOWS)
    return pl.pallas_call(
        functools.partial(_pool_sample_kernel, n_new=n_new, gw=gw),
        out_shape=jax.ShapeDtypeStruct((n_new, bd, pw), BF16),
        compiler_params=pltpu.CompilerParams(vmem_limit_bytes=_vmem_limit(2 * rows * bd * pw * 4)),
        name="pool_sample",
    )(ext_t, wmix, scale)


def _merge_route_kernel(attn_ref, pool_ref, sga_ref, sgp_ref, x_ref, wa_ref, wp_ref, wo_ref,
                        gain_ref, wr_ref, br_ref, cnt_in_ref,
                        h_ref, tok_ref, idx_ref, rank_ref, gate_ref, cnt_out_ref, cnt_ref):
    i = pl.program_id(0)
    tm = x_ref.shape[0]
    n_e = wr_ref.shape[0]

    @pl.when(i == 0)
    def _():
        cnt_ref[...] = cnt_in_ref[...]

    a = jnp.dot(attn_ref[...], wa_ref[...], preferred_element_type=F32)
    p = jnp.dot(pool_ref[...], wp_ref[...], preferred_element_type=F32)
    mixed = sga_ref[...].astype(F32) * a + sgp_ref[...].astype(F32) * p
    h = x_ref[...] + jnp.dot(mixed.astype(BF16), wo_ref[...], preferred_element_type=F32)
    h_ref[...] = h
    tok = _rms(h, gain_ref[...])
    tok_ref[...] = tok

    logits = lax.dot_general(wr_ref[...], tok.astype(BF16), (((1,), (1,)), ((), ())),
                             preferred_element_type=F32) + br_ref[...]
    eid = lax.broadcasted_iota(I32, (n_e, tm), 0).astype(F32)
    work = logits
    member = jnp.zeros((n_e, tm), F32)
    vals, idxs = [], []
    for _ in range(TOP_K):
        mk = jnp.max(work, axis=0, keepdims=True)
        ik = jnp.min(jnp.where(work == mk, eid, float(n_e)), axis=0, keepdims=True)
        sel = eid == ik
        vals.append(mk)
        idxs.append(ik)
        member = member + sel.astype(F32)
        work = jnp.where(sel, -jnp.inf, work)
    ex = [jnp.exp(v - vals[0]) for v in vals]
    den = ex[0]
    for e in ex[1:]:
        den = den + e
    rr = lax.broadcasted_iota(I32, (tm, tm), 0)
    cc = lax.broadcasted_iota(I32, (tm, tm), 1)
    earlier = (rr < cc).astype(BF16)
    before = jnp.dot(member.astype(BF16), earlier, preferred_element_type=F32) + cnt_ref[...]
    ranks = [jnp.sum(jnp.where(eid == ik, before, 0.0), axis=0, keepdims=True) for ik in idxs]
    cnt_ref[...] = cnt_ref[...] + jnp.sum(member, axis=1, keepdims=True)

    krow = lax.broadcasted_iota(I32, (TOP_K, tm), 0)

    def rows(parts):
        out = jnp.zeros((TOP_K, tm), F32)
        for k, c in enumerate(parts):
            out = jnp.where(krow == k, c, out)
        return out

    idx_ref[...] = rows(idxs).astype(I32)
    rank_ref[...] = rows(ranks).astype(I32)
    gate_ref[...] = rows([e / den for e in ex])
    cnt_out_ref[...] = cnt_ref[...]


def _merge_route(attn, pool_o, sg, x, wa, wp, wo, gain, wr, br, cnt_in, tm):
    t, d = x.shape
    qw, pw, n_e = attn.shape[1], pool_o.shape[1], wr.shape[0]
    tm = min(tm, t)
    assert t % tm == 0
    per_tok = pl.BlockSpec((TOP_K, tm), lambda i: (0, i))
    row = lambda w: pl.BlockSpec((tm, w), lambda i: (i, 0))
    whole = lambda a: pl.BlockSpec(a.shape, lambda i: (0,) * a.ndim, pipeline_mode=pl.Buffered(1))
    out_shape = [
        jax.ShapeDtypeStruct((t, d), F32),
        jax.ShapeDtypeStruct((t, d), F32),
        jax.ShapeDtypeStruct((TOP_K, t), I32),
        jax.ShapeDtypeStruct((TOP_K, t), I32),
        jax.ShapeDtypeStruct((TOP_K, t), F32),
        jax.ShapeDtypeStruct((n_e, 1), F32),
    ]
    in_specs = [row(qw), row(pw), pl.BlockSpec((tm, d), lambda i: (i, 0)),
                pl.BlockSpec((tm, d), lambda i: (i, 1)), row(d),
                whole(wa), whole(wp), whole(wo), whole(gain), whole(wr), whole(br), whole(cnt_in)]
    est = (wa.size + wp.size + wo.size) * 2 + 2 * tm * (qw + pw + 2 * d) * 2 + 6 * tm * d * 4 + 8 * tm * d * 4
    return pl.pallas_call(
        _merge_route_kernel,
        out_shape=out_shape,
        grid=(t // tm,),
        in_specs=in_specs,
        out_specs=[row(d), row(d), per_tok, per_tok, per_tok,
                   pl.BlockSpec((n_e, 1), lambda i: (0, 0))],
        scratch_shapes=[pltpu.VMEM((n_e, 1), F32)],
        compiler_params=pltpu.CompilerParams(
            dimension_semantics=("arbitrary",), vmem_limit_bytes=_vmem_limit(est)),
        name="merge_route",
    )(attn, pool_o, sg, sg, x, wa, wp, wo, gain, wr, br, cnt_in)


def _dispatch_kernel(dest_a_ref, dest_b_ref, pad_off_ref, pad_len_ref, used_ref, meta_ref,
                     tok_a_ref, tok_b_ref, xs_ref, zero_ref, sem, pad_sem, chunk_sem, *,
                     tm, t_a, t_b, n_e):
    i = pl.program_id(0)
    n_a = t_a // tm
    p = zero_ref.shape[0]
    n_chunks = xs_ref.shape[0] // p

    def pad_copy(d):
        return pltpu.make_async_copy(zero_ref.at[pl.ds(0, 1)], xs_ref.at[pl.ds(d, 1)], pad_sem)

    def chunk_copy(b):
        return pltpu.make_async_copy(zero_ref, xs_ref.at[pl.ds(pl.multiple_of(b * p, p), p)], chunk_sem)

    @pl.when(i == 0)
    def _():
        zero_ref[...] = jnp.zeros_like(zero_ref)

        def pad_group(e, c):
            off = pad_off_ref[e]

            def one(r, c2):
                pad_copy(off + r).start()
                return c2

            lax.fori_loop(0, pad_len_ref[e], one, 0)
            return c

        lax.fori_loop(0, n_e, pad_group, 0)

        def chunk(b, c):
            @pl.when(used_ref[b] == 0)
            def _():
                chunk_copy(b).start()

            return c

        lax.fori_loop(0, n_chunks, chunk, 0)

    def scatter(tok_ref, dest_ref, t, base):
        def row_copy(r, d):
            return pltpu.make_async_copy(tok_ref.at[pl.ds(r, 1)], xs_ref.at[pl.ds(d, 1)], sem)

        def issue(r, c):
            for k in range(TOP_K):
                row_copy(r, dest_ref[(base + r) * TOP_K + k]).start()
            return c

        lax.fori_loop(0, tm, issue, 0)

        def drain(r, c):
            for k in range(TOP_K):
                row_copy(r, 0).wait()
            return c

        lax.fori_loop(0, tm, drain, 0)

    @pl.when(i < n_a)
    def _():
        scatter(tok_a_ref, dest_a_ref, t_a, i * tm)

    @pl.when(i >= n_a)
    def _():
        scatter(tok_b_ref, dest_b_ref, t_b, (i - n_a) * tm)

    @pl.when(i == 0)
    def _():
        def pad_wait(r, c):
            pad_copy(0).wait()
            return c

        lax.fori_loop(0, meta_ref[0], pad_wait, 0)

        def chunk_wait(b, c):
            chunk_copy(0).wait()
            return c

        lax.fori_loop(0, meta_ref[1], chunk_wait, 0)


def _dispatch(dest_a, dest_b, pad_off, pad_len, used, meta, tok_a, tok_b, n_slots, tm):
    (t_a, d), t_b = tok_a.shape, tok_b.shape[0]
    assert t_a % tm == 0 and t_b % tm == 0 and n_slots % EXPERT_CHUNK == 0
    n_a, n_b = t_a // tm, t_b // tm
    n_e = pad_off.shape[0]
    return pl.pallas_call(
        functools.partial(_dispatch_kernel, tm=tm, t_a=t_a, t_b=t_b, n_e=n_e),
        out_shape=jax.ShapeDtypeStruct((n_slots, d), F32),
        grid_spec=pltpu.PrefetchScalarGridSpec(
            num_scalar_prefetch=6,
            grid=(n_a + n_b,),
            in_specs=[pl.BlockSpec((tm, d), lambda i, *_: (jnp.minimum(i, n_a - 1), 0)),
                      pl.BlockSpec((tm, d), lambda i, *_: (jnp.maximum(i - n_a, 0), 0))],
            out_specs=pl.BlockSpec(memory_space=pl.ANY),
            scratch_shapes=[pltpu.VMEM((EXPERT_CHUNK, d), F32), pltpu.SemaphoreType.DMA(()),
                            pltpu.SemaphoreType.DMA(()), pltpu.SemaphoreType.DMA(())],
        ),
        compiler_params=pltpu.CompilerParams(
            dimension_semantics=("arbitrary",), has_side_effects=True),
        name="dispatch",
    )(dest_a, dest_b, pad_off, pad_len, used, meta, tok_a, tok_b)


def _is_new_expert(be_ref, i):
    return jnp.logical_or(i == 0, be_ref[i] != be_ref[jnp.maximum(i - 1, 0)])


def _per_used_chunks(nv, rows, compute, zero_fill):
    n_chunks = rows // EXPERT_CHUNK
    for n in range(n_chunks + 1):
        @pl.when(nv == n)
        def _(n=n):
            if n:
                compute(n * EXPERT_CHUNK)
            if n < n_chunks:
                zero_fill(n * EXPERT_CHUNK)


def _expert_up_kernel(be_ref, nv_ref, xs_ref, wg_ref, wu_ref, bg_ref, bu_ref, h_ref, wg_bf, wu_bf):
    i = pl.program_id(1)

    @pl.when(_is_new_expert(be_ref, i))
    def _():
        wg_bf[...] = wg_ref[0].astype(BF16)
        wu_bf[...] = wu_ref[0].astype(BF16)

    def compute(m):
        x = xs_ref[0:m, :].astype(BF16)
        tf = h_ref.shape[1]
        cw = min(tf, 256)
        for c in range(tf // cw):
            cs = slice(c * cw, (c + 1) * cw)
            glu = jnp.dot(x, wg_bf[:, cs], preferred_element_type=F32) + bg_ref[0, :, cs]
            lin = jnp.dot(x, wu_bf[:, cs], preferred_element_type=F32) + bu_ref[0, :, cs]
            glu = jnp.minimum(glu, SWIGLU_LIMIT)
            lin = jnp.clip(lin, -SWIGLU_LIMIT, SWIGLU_LIMIT)
            h_ref[0:m, cs] = (glu * jax.nn.sigmoid(SWIGLU_ALPHA * glu) * (lin + 1.0)).astype(h_ref.dtype)

    def zero_fill(r0):
        h_ref[r0:, :] = jnp.zeros((h_ref.shape[0] - r0, h_ref.shape[1]), h_ref.dtype)

    _per_used_chunks(nv_ref[i], h_ref.shape[0], compute, zero_fill)


def _expert_up(blk_e, blk_nv, xs, wg, wu, bg, bu, tf):
    ns, d = xs.shape
    n_e, _, dff = wg.shape
    p = EXPERT_ROWS
    tf = min(tf, dff)
    assert ns % p == 0 and dff % tf == 0
    est = 2 * 2 * d * tf * 4 + 2 * d * tf * 2 + 2 * p * d * 4 + 2 * p * tf * 2 + 4 * p * tf * 4
    wspec = pl.BlockSpec((1, d, tf), lambda j, i, be, ok: (be[i], 0, j))
    bspec = pl.BlockSpec((1, 1, tf), lambda j, i, be, ok: (be[i], 0, j))
    return pl.pallas_call(
        _expert_up_kernel,
        out_shape=jax.ShapeDtypeStruct((ns, dff), BF16),
        grid_spec=pltpu.PrefetchScalarGridSpec(
            num_scalar_prefetch=2,
            grid=(dff // tf, ns // p),
            in_specs=[pl.BlockSpec((p, d), lambda j, i, be, ok: (i, 0)), wspec, wspec, bspec, bspec],
            out_specs=pl.BlockSpec((p, tf), lambda j, i, be, ok: (i, j)),
            scratch_shapes=[pltpu.VMEM((d, tf), BF16), pltpu.VMEM((d, tf), BF16)],
        ),
        compiler_params=pltpu.CompilerParams(
            dimension_semantics=("arbitrary", "arbitrary"), vmem_limit_bytes=_vmem_limit(est)),
        name="expert_up",
    )(blk_e, blk_nv, xs, wg, wu, bg.reshape(n_e, 1, dff), bu.reshape(n_e, 1, dff))


def _expert_down_kernel(be_ref, nv_ref, h_ref, wd_ref, bd_ref, y_ref, wd_bf):
    i = pl.program_id(1)

    @pl.when(_is_new_expert(be_ref, i))
    def _():
        wd_bf[...] = wd_ref[0].astype(BF16)

    def compute(m):
        y_ref[0:m, :] = jnp.dot(h_ref[0:m, :], wd_bf[...], preferred_element_type=F32) + bd_ref[0]

    def zero_fill(r0):
        y_ref[r0:, :] = jnp.zeros((y_ref.shape[0] - r0, y_ref.shape[1]), y_ref.dtype)

    _per_used_chunks(nv_ref[i], y_ref.shape[0], compute, zero_fill)


def _expert_down(blk_e, blk_nv, hmid, wd, bd, tn):
    ns, dff = hmid.shape
    n_e, _, d = wd.shape
    p = EXPERT_ROWS
    tn = min(tn, d)
    assert ns % p == 0 and d % tn == 0
    est = 2 * dff * tn * 4 + dff * tn * 2 + 2 * p * dff * 2 + 2 * p * tn * 4 + 2 * p * tn * 4
    return pl.pallas_call(
        _expert_down_kernel,
        out_shape=jax.ShapeDtypeStruct((ns, d), F32),
        grid_spec=pltpu.PrefetchScalarGridSpec(
            num_scalar_prefetch=2,
            grid=(d // tn, ns // p),
            in_specs=[pl.BlockSpec((p, dff), lambda j, i, be, ok: (i, 0)),
                      pl.BlockSpec((1, dff, tn), lambda j, i, be, ok: (be[i], 0, j)),
                      pl.BlockSpec((1, 1, tn), lambda j, i, be, ok: (be[i], 0, j))],
            out_specs=pl.BlockSpec((p, tn), lambda j, i, be, ok: (i, j)),
            scratch_shapes=[pltpu.VMEM((dff, tn), BF16)],
        ),
        compiler_params=pltpu.CompilerParams(
            dimension_semantics=("arbitrary", "arbitrary"), vmem_limit_bytes=_vmem_limit(est)),
        name="expert_down",
    )(blk_e, blk_nv, hmid, wd, bd.reshape(n_e, 1, d))


def _combine_kernel(dest_ref, h_ref, gate_ref, gain_ref, ys_ref, o_ref, buf_ref, sem, *, tm, t):
    base = pl.program_id(0) * tm

    def row_copy(r, k, d):
        return pltpu.make_async_copy(ys_ref.at[pl.ds(d, 1)], buf_ref.at[k, pl.ds(r, 1)], sem)

    def issue(r, c):
        for k in range(TOP_K):
            row_copy(r, k, dest_ref[(base + r) * TOP_K + k]).start()
        return c

    lax.fori_loop(0, tm, issue, 0)

    def drain(r, c):
        for k in range(TOP_K):
            row_copy(r, k, 0).wait()
        return c

    lax.fori_loop(0, tm, drain, 0)

    acc = h_ref[...]
    gates = gate_ref[...]
    for k in range(TOP_K):
        acc = acc + gates[:, k:k + 1] * buf_ref[k]
    o_ref[...] = _rms(acc, gain_ref[...])


def _combine(dest_flat, h, gates, gain, ys, tm):
    t, d = h.shape
    tm = min(tm, t)
    assert t % tm == 0
    est = TOP_K * tm * d * 4 + 4 * tm * d * 4 + 2 * tm * d * 4
    return pl.pallas_call(
        functools.partial(_combine_kernel, tm=tm, t=t),
        out_shape=jax.ShapeDtypeStruct((t, d), F32),
        grid_spec=pltpu.PrefetchScalarGridSpec(
            num_scalar_prefetch=1,
            grid=(t // tm,),
            in_specs=[pl.BlockSpec((tm, d), lambda i, dest: (i, 0)),
                      pl.BlockSpec((tm, TOP_K), lambda i, dest: (i, 0)),
                      pl.BlockSpec((1, d), lambda i, dest: (0, 0)),
                      pl.BlockSpec(memory_space=pl.ANY)],
            out_specs=pl.BlockSpec((tm, d), lambda i, dest: (i, 0)),
            scratch_shapes=[pltpu.VMEM((TOP_K, tm, d), F32), pltpu.SemaphoreType.DMA(())],
        ),
        compiler_params=pltpu.CompilerParams(
            dimension_semantics=("arbitrary",), vmem_limit_bytes=_vmem_limit(est)),
        name="combine",
    )(dest_flat, h, gates, gain, ys)


def kernel(x_prompt, x_sample, cache_k, cache_v, state_pool, norm_mix, w_in, attn_sinks, w_pool_mix,
           pool_scale, w_attn_out, w_pool_out, w_out, norm_ffn, w_router, b_router, w_gate, b_gate,
           w_up, b_up, w_down, b_down, norm_final):
    assert norm_mix.shape[0] == 1, "single-layer step"
    bp, sp, d = x_prompt.shape
    bs, ns, _ = x_sample.shape
    n_kv, hd = cache_k.shape[-2:]
    n_q = attn_sinks.shape[1]
    qw, kvw = n_q * hd, n_kv * hd
    pw = state_pool.shape[-1]
    n_e = w_router.shape[-1]
    tp_, ts_ = bp * sp, bs * ns
    t_all = tp_ + ts_

    gain_mix = norm_mix[0].reshape(1, d)
    w_in_bf = w_in[0].astype(BF16)
    w_q = w_in_bf[:, :qw]
    w_kvu = w_in_bf[:, qw:qw + 2 * kvw + pw]
    w_g = w_in_bf[:, qw + 2 * kvw + pw:]
    wmix = w_pool_mix[0].astype(BF16)
    pscale = pool_scale[0].reshape(1, pw)
    wa = w_attn_out[0].astype(BF16)
    wp = w_pool_out[0].astype(BF16)
    wo = w_out[0].astype(BF16)
    sinks = attn_sinks[0].astype(F32)

    def project(x2d):
        xn = _rms_cast(x2d, gain_mix, 512)
        q = _proj(xn, w_q, BF16, "scale", 1024, 1024, "proj_q")
        kvu = _proj(xn, w_kvu, F32, "none", 1024, 768, "proj_kvu")
        sg = _proj(xn, w_g, BF16, "sigmoid", 1024, 1024, "proj_gates")
        return q, kvu, sg

    xp = x_prompt.reshape(tp_, d)
    q_p, kvu_p, sg_p = project(xp)
    attn_p = _attn_prompt(q_p, kvu_p, sinks, sp, n_kv, hd)
    pool_p = _pool_prompt(kvu_p, 2 * kvw, pw, wmix, pscale, sp, 512)

    xs_ = x_sample.reshape(ts_, d)
    q_s, kvu_s, sg_s = project(xs_)
    k_s = kvu_s[:, :kvw].reshape(bs, ns, kvw)
    v_s = kvu_s[:, kvw:2 * kvw].reshape(bs, ns, kvw)
    u_s = kvu_s[:, 2 * kvw:].reshape(bs, ns, pw)
    ck = cache_k[0].reshape(bs, -1, kvw)
    cv = cache_v[0].reshape(bs, -1, kvw)
    o_s = _attn_sample(q_s.reshape(bs, ns, qw), k_s, v_s, ck, cv, sinks, n_kv, hd)
    attn_s = (o_s.reshape(bs, n_kv, n_q // n_kv, ns, hd).transpose(0, 3, 1, 2, 4)
              .reshape(ts_, qw).astype(BF16))
    ext = jnp.concatenate([state_pool[0], u_s], axis=1)
    pool_s = _pool_sample(ext.transpose(1, 0, 2), ns, wmix, pscale)
    pool_s = pool_s.transpose(1, 0, 2).reshape(ts_, pw)

    gain_ffn = norm_ffn[0].reshape(1, d)
    wr = w_router[0].T.astype(BF16)
    br = b_router[0].reshape(n_e, 1).astype(F32)
    tm = 256
    h_p, tok_p, idx_p, rank_p, gates_p, cnt_p = _merge_route(
        attn_p, pool_p, sg_p, xp, wa, wp, wo, gain_ffn, wr, br, jnp.zeros((n_e, 1), F32), tm)
    h_s, tok_s, idx_s, rank_s, gates_s, counts = _merge_route(
        attn_s, pool_s, sg_s, xs_, wa, wp, wo, gain_ffn, wr, br, cnt_p, tm)

    p = EXPERT_ROWS
    cnt = counts[:, 0].astype(I32)
    padded = (cnt + p - 1) // p * p
    pad_end = jnp.cumsum(padded)
    pad_start = pad_end - padded
    experts = jnp.arange(n_e, dtype=I32)[:, None, None]

    def slot_ids(idx, rank):
        start = jnp.sum(jnp.where(idx[None] == experts, pad_start[:, None, None], 0), axis=0)
        return (start + rank).T.reshape(-1)

    dest_p = slot_ids(idx_p, rank_p)
    dest_s = slot_ids(idx_s, rank_s)
    n_slots = -(-(t_all * TOP_K) // p) * p + n_e * p
    n_tiles = n_slots // p
    tile_start = jnp.arange(n_tiles, dtype=I32) * p
    blk_ok = (tile_start < pad_end[-1]).astype(I32)
    blk_e = jnp.sum((pad_end[None, :] <= tile_start[:, None]).astype(I32), axis=1)
    blk_e = jnp.minimum(blk_e, n_e - 1)
    blk_e = jnp.where(blk_ok == 1, blk_e, jnp.max(blk_e * blk_ok))
    ck = EXPERT_CHUNK
    chunk_start = jnp.arange(n_slots // ck, dtype=I32)[:, None] * ck
    used = jnp.any((pad_start[None, :] <= chunk_start) & (chunk_start < (pad_start + cnt)[None, :]),
                   axis=1).astype(I32)
    blk_nv = jnp.sum(used.reshape(n_tiles, p // ck), axis=1)
    pad_len = (cnt + ck - 1) // ck * ck - cnt

    meta = jnp.stack([jnp.sum(pad_len), used.shape[0] - jnp.sum(used)]).astype(I32)
    xs_sorted = _dispatch(dest_p, dest_s, pad_start + cnt, pad_len, used, meta, tok_p, tok_s, n_slots, tm)
    hmid = _expert_up(blk_e, blk_nv, xs_sorted, w_gate[0], w_up[0], b_gate[0], b_up[0], 1024)
    ys = _expert_down(blk_e, blk_nv, hmid, w_down[0], b_down[0], 1024)

    gain_fin = norm_final.reshape(1, d)
    y_p = _combine(dest_p, h_p, gates_p.T, gain_fin, ys, tm)
    y_s = _combine(dest_s, h_s, gates_s.T, gain_fin, ys, tm)

    keep = min(WINDOW, sp)
    tail_p = kvu_p.reshape(bp, sp, 2 * kvw + pw)[:, sp - keep:]
    n_rows = cache_k.shape[2]
    new_k_s = jnp.concatenate([cache_k[0], k_s.reshape(bs, ns, n_kv, hd)], axis=1)[:, -n_rows:]
    new_v_s = jnp.concatenate([cache_v[0], v_s.reshape(bs, ns, n_kv, hd)], axis=1)[:, -n_rows:]
    return (y_p.reshape(bp, sp, d), y_s.reshape(bs, ns, d),
            tail_p[:, :, :kvw].reshape(1, bp, keep, n_kv, hd),
            tail_p[:, :, kvw:2 * kvw].reshape(1, bp, keep, n_kv, hd),
            tail_p[None, :, keep - POOL_STATE:, 2 * kvw:],
            new_k_s[None], new_v_s[None], ext[None, :, -POOL_STATE:])
```

```python
import functools

import jax
import jax.numpy as jnp
import numpy as np
from jax import lax
from jax.experimental import pallas as pl
from jax.experimental.pallas import tpu as pltpu

F32 = jnp.float32
BF16 = jnp.bfloat16
I32 = jnp.int32

WINDOW = 128
POOL_WINDOWS = (2, 4, 8, 16)
POOL_STATE = max(POOL_WINDOWS) - 1
PAST_LEN = 16384
TOP_K = 4
SWIGLU_LIMIT = 7.0
SWIGLU_ALPHA = 1.702
RMS_EPS = 1e-5
NEG_BIG = -1e30
LOG2E = 1.4426950408889634

V7X_VMEM_BYTES = 64 * 1024 * 1024
EXPERT_ROWS = 512
EXPERT_CHUNK = 256
HALO = 16


def _vmem_limit(nbytes):
    return int(min(nbytes + (8 << 20), V7X_VMEM_BYTES - (4 << 20)))


def _rms(x, gain):
    ms = jnp.mean(x * x, axis=-1, keepdims=True)
    return x * lax.rsqrt(ms + RMS_EPS) * gain


def _rms_cast_kernel(x_ref, g_ref, o_ref):
    o_ref[...] = _rms(x_ref[...], g_ref[...]).astype(o_ref.dtype)


def _rms_cast(x, gain, tm):
    t, d = x.shape
    tm = min(tm, t)
    assert t % tm == 0
    return pl.pallas_call(
        _rms_cast_kernel,
        out_shape=jax.ShapeDtypeStruct((t, d), BF16),
        grid=(t // tm,),
        in_specs=[pl.BlockSpec((tm, d), lambda i: (i, 0)), pl.BlockSpec((1, d), lambda i: (0, 0))],
        out_specs=pl.BlockSpec((tm, d), lambda i: (i, 0)),
        compiler_params=pltpu.CompilerParams(
            dimension_semantics=("parallel",), vmem_limit_bytes=_vmem_limit(16 * tm * d)),
        name="rms_cast",
    )(x, gain)


def _proj_kernel(xn_ref, w_ref, o_ref, *, epilogue):
    acc = jnp.dot(xn_ref[...], w_ref[...], preferred_element_type=F32)
    if epilogue == "sigmoid":
        acc = jax.nn.sigmoid(acc)
    elif epilogue == "scale":
        acc = acc * 0.125
    o_ref[...] = acc.astype(o_ref.dtype)


def _proj(xn, w, out_dtype, epilogue, tm, tn, name):
    t, d = xn.shape
    n = w.shape[1]
    tm = min(tm, t)
    tn = min(tn, n)
    assert t % tm == 0 and n % tn == 0
    est = 2 * tm * d * 2 + 2 * d * tn * 2 + 2 * tm * tn * 4 + 2 * tm * tn * 4
    return pl.pallas_call(
        functools.partial(_proj_kernel, epilogue=epilogue),
        out_shape=jax.ShapeDtypeStruct((t, n), out_dtype),
        grid=(n // tn, t // tm),
        in_specs=[
            pl.BlockSpec((tm, d), lambda j, i: (i, 0)),
            pl.BlockSpec((d, tn), lambda j, i: (0, j)),
        ],
        out_specs=pl.BlockSpec((tm, tn), lambda j, i: (i, j)),
        compiler_params=pltpu.CompilerParams(
            dimension_semantics=("parallel", "parallel"),
            vmem_limit_bytes=_vmem_limit(est)),
        name=name,
    )(xn, w)


def _attn_prompt_kernel(sink_ref, bias_ref, q_ref, kc_ref, vc_ref, kp_ref, vp_ref, o_ref, *,
                        n_kv, q_per_kv, hd, blocks_per_seq):
    blk = WINDOW
    rq = q_per_kv
    nt = (((1,), (1,)), ((), ()))
    has_prev = (pl.program_id(0) % blocks_per_seq) > 0
    kj = lax.broadcasted_iota(I32, (blk, 2 * blk), 1)
    prev_ok = (kj >= blk) | has_prev

    def scores(g):
        cs = slice(g * hd, (g + 1) * hd)
        k = (jnp.concatenate([kp_ref[:, cs], kc_ref[:, cs]], axis=0) * LOG2E).astype(BF16)
        qs = jnp.concatenate([q_ref[:, (g * rq + r) * hd:(g * rq + r + 1) * hd] for r in range(rq)], axis=0)
        s = lax.dot_general(qs, k, nt, preferred_element_type=F32)
        slabs, maxes = [], []
        for r in range(rq):
            h = g * rq + r
            sr = jnp.where(prev_ok, s[r * blk:(r + 1) * blk] + bias_ref[h], NEG_BIG)
            slabs.append(sr)
            maxes.append(jnp.maximum(jnp.max(sr, axis=-1, keepdims=True), sink_ref[h] * LOG2E))
        return slabs, maxes

    def finish(g, slabs, maxes):
        cs = slice(g * hd, (g + 1) * hd)
        v = jnp.concatenate([vp_ref[:, cs], vc_ref[:, cs]], axis=0).astype(BF16)
        probs, inv = [], []
        for r in range(rq):
            p = jnp.exp2(slabs[r] - maxes[r])
            denom = jnp.sum(p, axis=-1, keepdims=True) + jnp.exp2(sink_ref[g * rq + r] * LOG2E - maxes[r])
            probs.append(p.astype(BF16))
            inv.append(1.0 / denom)
        o = jnp.dot(jnp.concatenate(probs, axis=0), v, preferred_element_type=F32)
        for r in range(rq):
            h = g * rq + r
            o_ref[:, h * hd:(h + 1) * hd] = (o[r * blk:(r + 1) * blk] * inv[r]).astype(o_ref.dtype)

    pending = scores(0)
    for g in range(n_kv):
        nxt = scores(g + 1) if g + 1 < n_kv else None
        finish(g, *pending)
        pending = nxt


def _attn_prompt(q, kvu, sinks, seq, n_kv, hd):
    t, qw = q.shape
    n_q = qw // hd
    kvw = n_kv * hd
    blk = WINDOW
    assert t % blk == 0 and seq % blk == 0
    slopes = 2.0 ** (-8.0 * np.arange(1, n_q + 1, dtype=np.float64) / n_q)
    dist = (np.arange(blk)[:, None] + blk - np.arange(2 * blk)[None, :]).astype(np.float64)
    band = (dist >= 0) & (dist <= WINDOW)
    bias = np.where(band[None], -LOG2E * slopes[:, None, None] * dist[None], NEG_BIG).astype(np.float32)
    prev = lambda i: jnp.maximum(i - 1, 0)
    return pl.pallas_call(
        functools.partial(_attn_prompt_kernel, n_kv=n_kv, q_per_kv=n_q // n_kv, hd=hd,
                          blocks_per_seq=seq // blk),
        out_shape=jax.ShapeDtypeStruct((t, qw), BF16),
        grid=(t // blk,),
        in_specs=[
            pl.BlockSpec(memory_space=pltpu.SMEM),
            pl.BlockSpec((n_q, blk, 2 * blk), lambda i: (0, 0, 0)),
            pl.BlockSpec((blk, qw), lambda i: (i, 0)),
            pl.BlockSpec((blk, kvw), lambda i: (i, 0)),
            pl.BlockSpec((blk, kvw), lambda i: (i, 1)),
            pl.BlockSpec((blk, kvw), lambda i: (prev(i), 0)),
            pl.BlockSpec((blk, kvw), lambda i: (prev(i), 1)),
        ],
        out_specs=pl.BlockSpec((blk, qw), lambda i: (i, 0)),
        compiler_params=pltpu.CompilerParams(dimension_semantics=("parallel",)),
        name="attn_prompt",
    )(sinks, jnp.asarray(bias), q, kvu, kvu, kvu, kvu)


def _attn_sample_kernel(qbd_ref, ck_ref, cv_ref, kn_ref, vn_ref, bc_ref, bn_ref, sink_ref, o_ref, *,
                        bb, n_kv, hd):
    nt = (((1,), (1,)), ((), ()))
    rows = qbd_ref.shape[1]
    grp = lax.broadcasted_iota(I32, (rows, 1), 0) // (rows // n_kv)
    sink = sink_ref[...]
    for b in range(bb):
        qb = qbd_ref[b]
        s_c = lax.dot_general(qb, ck_ref[b].astype(BF16), nt, preferred_element_type=F32) + bc_ref[...]
        s_n = lax.dot_general(qb, kn_ref[b].astype(BF16), nt, preferred_element_type=F32) + bn_ref[...]
        m = jnp.maximum(jnp.maximum(jnp.max(s_c, axis=-1, keepdims=True),
                                    jnp.max(s_n, axis=-1, keepdims=True)), sink)
        p_c = jnp.exp(s_c - m)
        p_n = jnp.exp(s_n - m)
        denom = (jnp.sum(p_c, axis=-1, keepdims=True) + jnp.sum(p_n, axis=-1, keepdims=True)
                 + jnp.exp(sink - m))
        o = (jnp.dot(p_c.astype(BF16), cv_ref[b].astype(BF16), preferred_element_type=F32)
             + jnp.dot(p_n.astype(BF16), vn_ref[b].astype(BF16), preferred_element_type=F32))
        sel = jnp.zeros((rows, hd), F32)
        for g in range(n_kv):
            sel = sel + jnp.where(grp == g, o[:, g * hd:(g + 1) * hd], 0.0)
        o_ref[b] = sel / denom


def _attn_sample(q_s, k_new, v_new, cache_k, cache_v, sinks, n_kv, hd):
    bd, n, qw = q_s.shape
    n_q = qw // hd
    r = n_q // n_kv
    kvw = n_kv * hd
    w = cache_k.shape[1]
    rows = n_q * n
    npad = 16
    q5 = q_s.reshape(bd, n, n_kv, r, hd)
    qbd = jnp.einsum("bigrd,gh->bgrihd", q5, jnp.eye(n_kv, dtype=q_s.dtype)).reshape(bd, rows, kvw)
    pad = ((0, 0), (0, npad - n), (0, 0))
    k_new = jnp.pad(k_new, pad)
    v_new = jnp.pad(v_new, pad)
    slopes = 2.0 ** (-8.0 * np.arange(1, n_q + 1, dtype=np.float64) / n_q)
    slope_c = np.repeat(slopes, n)
    i_c = np.tile(np.arange(n), n_q)
    dist_c = (w + i_c)[:, None] - np.arange(w)[None, :]
    bias_c = np.where((dist_c >= 0) & (dist_c <= WINDOW), -slope_c[:, None] * dist_c, NEG_BIG)
    dist_n = i_c[:, None] - np.arange(npad)[None, :]
    ok_n = (dist_n >= 0) & (np.arange(npad)[None, :] < n)
    bias_n = np.where(ok_n, -slope_c[:, None] * dist_n, NEG_BIG)
    sink_c = jnp.repeat(sinks.astype(F32), n).reshape(rows, 1)
    bb = 8 if bd % 8 == 0 else 1
    return pl.pallas_call(
        functools.partial(_attn_sample_kernel, bb=bb, n_kv=n_kv, hd=hd),
        out_shape=jax.ShapeDtypeStruct((bd, rows, hd), F32),
        grid=(bd // bb,),
        in_specs=[
            pl.BlockSpec((bb, rows, kvw), lambda i: (i, 0, 0)),
            pl.BlockSpec((bb, w, kvw), lambda i: (i, 0, 0)),
            pl.BlockSpec((bb, w, kvw), lambda i: (i, 0, 0)),
            pl.BlockSpec((bb, npad, kvw), lambda i: (i, 0, 0)),
            pl.BlockSpec((bb, npad, kvw), lambda i: (i, 0, 0)),
            pl.BlockSpec((rows, w), lambda i: (0, 0)),
            pl.BlockSpec((rows, npad), lambda i: (0, 0)),
            pl.BlockSpec((rows, 1), lambda i: (0, 0)),
        ],
        out_specs=pl.BlockSpec((bb, rows, hd), lambda i: (i, 0, 0)),
        compiler_params=pltpu.CompilerParams(dimension_semantics=("parallel",)),
        name="attn_sample",
    )(qbd, cache_k, cache_v, k_new, v_new,
      jnp.asarray(bias_c, F32), jnp.asarray(bias_n, F32), sink_c)


def _pool_prompt_kernel(*refs, tp, gw, tiles_per_seq):
    ng = len(POOL_WINDOWS)
    cur = refs[:ng]
    halo = refs[ng:2 * ng]
    wmix_ref, scale_ref, o_ref, ext_ref = refs[2 * ng:]
    tile_in_seq = pl.program_id(0) % tiles_per_seq
    first = tile_in_seq == 0
    pos = lax.broadcasted_iota(I32, (tp, 1), 0) + tile_in_seq * tp
    for g, w in enumerate(POOL_WINDOWS):
        u = cur[g][...]
        ext_ref[0:HALO, :] = jnp.where(first, 0.0, halo[g][...])
        ext_ref[HALO:HALO + tp, :] = u
        acc = u
        for d in range(1, w):
            acc = acc + ext_ref[HALO - d:HALO - d + tp, :]
        cnt = jnp.minimum(w, pos + 1).astype(F32)
        pooled = acc / cnt - u
        z = jnp.dot(pooled.astype(BF16), wmix_ref[g], preferred_element_type=F32)
        o_ref[:, g * gw:(g + 1) * gw] = (z * scale_ref[:, g * gw:(g + 1) * gw]).astype(o_ref.dtype)


def _pool_prompt(kvu, u_col0, pw, wmix, scale, seq, tp):
    t = kvu.shape[0]
    ng = len(POOL_WINDOWS)
    gw = pw // ng
    tp = min(tp, seq)
    assert seq % tp == 0 and tp % HALO == 0 and u_col0 % gw == 0
    c0 = u_col0 // gw
    hb = tp // HALO
    cur_specs = [pl.BlockSpec((tp, gw), functools.partial(lambda i, g: (i, c0 + g), g=g)) for g in range(ng)]
    halo_specs = [pl.BlockSpec((HALO, gw),
                               functools.partial(lambda i, g: (jnp.maximum(i * hb - 1, 0), c0 + g), g=g))
                  for g in range(ng)]
    return pl.pallas_call(
        functools.partial(_pool_prompt_kernel, tp=tp, gw=gw, tiles_per_seq=seq // tp),
        out_shape=jax.ShapeDtypeStruct((t, pw), BF16),
        grid=(t // tp,),
        in_specs=cur_specs + halo_specs + [
            pl.BlockSpec((ng, gw, gw), lambda i: (0, 0, 0)),
            pl.BlockSpec((1, pw), lambda i: (0, 0)),
        ],
        out_specs=pl.BlockSpec((tp, pw), lambda i: (i, 0)),
        scratch_shapes=[pltpu.VMEM((HALO + tp, gw), F32)],
        compiler_params=pltpu.CompilerParams(dimension_semantics=("parallel",)),
        name="pool_prompt",
    )(*([kvu] * (2 * ng)), wmix, scale)


def _pool_sample_kernel(ext_ref, wmix_ref, scale_ref, o_ref, *, n_new, gw):
    n_prev = ext_ref.shape[0] - n_new
    for i in range(n_new):
        for g, w in enumerate(POOL_WINDOWS):
            cs = slice(g * gw, (g + 1) * gw)
            u = ext_ref[n_prev + i, :, cs]
            acc = u
            for d in range(1, w):
                acc = acc + ext_ref[n_prev + i - d, :, cs]
            cnt = float(min(w, PAST_LEN + i + 1))
            pooled = acc / cnt - u
            z = jnp.dot(pooled.astype(BF16), wmix_ref[g], preferred_element_type=F32)
            o_ref[i, :, cs] = (z * scale_ref[:, cs]).astype(o_ref.dtype)


def _pool_sample(ext_t, n_new, wmix, scale):
    rows, bd, pw = ext_t.shape
    gw = pw // len(POOL_WINDOWS)
    return pl.pallas_call(
        functools.partial(_pool_sample_kernel, n_new=n_new, gw=gw),
        out_shape=jax.ShapeDtypeStruct((n_new, bd, pw), BF16),
        compiler_params=pltpu.CompilerParams(vmem_limit_bytes=_vmem_limit(2 * rows * bd * pw * 4)),
        name="pool_sample",
    )(ext_t, wmix, scale)


def _merge_route_kernel(attn_ref, pool_ref, sga_ref, sgp_ref, x_ref, wa_ref, wp_ref, wo_ref,
                        gain_ref, wr_ref, br_ref, cnt_in_ref,
                        h_ref, tok_ref, idx_ref, rank_ref, gate_ref, cnt_out_ref, cnt_ref):
    i = pl.program_id(0)
    tm = x_ref.shape[0]
    n_e = wr_ref.shape[0]

    @pl.when(i == 0)
    def _():
        cnt_ref[...] = cnt_in_ref[...]

    a = jnp.dot(attn_ref[...], wa_ref[...], preferred_element_type=F32)
    p = jnp.dot(pool_ref[...], wp_ref[...], preferred_element_type=F32)
    mixed = sga_ref[...].astype(F32) * a + sgp_ref[...].astype(F32) * p
    h = x_ref[...] + jnp.dot(mixed.astype(BF16), wo_ref[...], preferred_element_type=F32)
    h_ref[...] = h
    tok = _rms(h, gain_ref[...])
    tok_ref[...] = tok

    logits = lax.dot_general(wr_ref[...], tok.astype(BF16), (((1,), (1,)), ((), ())),
                             preferred_element_type=F32) + br_ref[...]
    eid = lax.broadcasted_iota(I32, (n_e, tm), 0).astype(F32)
    work = logits
    member = jnp.zeros((n_e, tm), F32)
    vals, idxs = [], []
    for _ in range(TOP_K):
        mk = jnp.max(work, axis=0, keepdims=True)
        ik = jnp.min(jnp.where(work == mk, eid, float(n_e)), axis=0, keepdims=True)
        sel = eid == ik
        vals.append(mk)
        idxs.append(ik)
        member = member + sel.astype(F32)
        work = jnp.where(sel, -jnp.inf, work)
    ex = [jnp.exp(v - vals[0]) for v in vals]
    den = ex[0]
    for e in ex[1:]:
        den = den + e
    rr = lax.broadcasted_iota(I32, (tm, tm), 0)
    cc = lax.broadcasted_iota(I32, (tm, tm), 1)
    earlier = (rr < cc).astype(BF16)
    before = jnp.dot(member.astype(BF16), earlier, preferred_element_type=F32) + cnt_ref[...]
    ranks = [jnp.sum(jnp.where(eid == ik, before, 0.0), axis=0, keepdims=True) for ik in idxs]
    cnt_ref[...] = cnt_ref[...] + jnp.sum(member, axis=1, keepdims=True)

    krow = lax.broadcasted_iota(I32, (TOP_K, tm), 0)

    def rows(parts):
        out = jnp.zeros((TOP_K, tm), F32)
        for k, c in enumerate(parts):
            out = jnp.where(krow == k, c, out)
        return out

    idx_ref[...] = rows(idxs).astype(I32)
    rank_ref[...] = rows(ranks).astype(I32)
    gate_ref[...] = rows([e / den for e in ex])
    cnt_out_ref[...] = cnt_ref[...]


def _merge_route(attn, pool_o, sg, x, wa, wp, wo, gain, wr, br, cnt_in, tm):
    t, d = x.shape
    qw, pw, n_e = attn.shape[1], pool_o.shape[1], wr.shape[0]
    tm = min(tm, t)
    assert t % tm == 0
    per_tok = pl.BlockSpec((TOP_K, tm), lambda i: (0, i))
    row = lambda w: pl.BlockSpec((tm, w), lambda i: (i, 0))
    whole = lambda a: pl.BlockSpec(a.shape, lambda i: (0,) * a.ndim, pipeline_mode=pl.Buffered(1))
    out_shape = [
        jax.ShapeDtypeStruct((t, d), F32),
        jax.ShapeDtypeStruct((t, d), F32),
        jax.ShapeDtypeStruct((TOP_K, t), I32),
        jax.ShapeDtypeStruct((TOP_K, t), I32),
        jax.ShapeDtypeStruct((TOP_K, t), F32),
        jax.ShapeDtypeStruct((n_e, 1), F32),
    ]
    in_specs = [row(qw), row(pw), pl.BlockSpec((tm, d), lambda i: (i, 0)),
                pl.BlockSpec((tm, d), lambda i: (i, 1)), row(d),
                whole(wa), whole(wp), whole(wo), whole(gain), whole(wr), whole(br), whole(cnt_in)]
    est = (wa.size + wp.size + wo.size) * 2 + 2 * tm * (qw + pw + 2 * d) * 2 + 6 * tm * d * 4 + 8 * tm * d * 4
    return pl.pallas_call(
        _merge_route_kernel,
        out_shape=out_shape,
        grid=(t // tm,),
        in_specs=in_specs,
        out_specs=[row(d), row(d), per_tok, per_tok, per_tok,
                   pl.BlockSpec((n_e, 1), lambda i: (0, 0))],
        scratch_shapes=[pltpu.VMEM((n_e, 1), F32)],
        compiler_params=pltpu.CompilerParams(
            dimension_semantics=("arbitrary",), vmem_limit_bytes=_vmem_limit(est)),
        name="merge_route",
    )(attn, pool_o, sg, sg, x, wa, wp, wo, gain, wr, br, cnt_in)


def _dispatch_kernel(dest_a_ref, dest_b_ref, pad_off_ref, pad_len_ref, used_ref, meta_ref,
                     tok_a_ref, tok_b_ref, xs_ref, zero_ref, sem, pad_sem, chunk_sem, *,
                     tm, t_a, t_b, n_e):
    i = pl.program_id(0)
    n_a = t_a // tm
    p = zero_ref.shape[0]
    n_chunks = xs_ref.shape[0] // p

    def pad_copy(d):
        return pltpu.make_async_copy(zero_ref.at[pl.ds(0, 1)], xs_ref.at[pl.ds(d, 1)], pad_sem)

    def chunk_copy(b):
        return pltpu.make_async_copy(zero_ref, xs_ref.at[pl.ds(pl.multiple_of(b * p, p), p)], chunk_sem)

    @pl.when(i == 0)
    def _():
        zero_ref[...] = jnp.zeros_like(zero_ref)

        def pad_group(e, c):
            off = pad_off_ref[e]

            def one(r, c2):
                pad_copy(off + r).start()
                return c2

            lax.fori_loop(0, pad_len_ref[e], one, 0)
            return c

        lax.fori_loop(0, n_e, pad_group, 0)

        def chunk(b, c):
            @pl.when(used_ref[b] == 0)
            def _():
                chunk_copy(b).start()

            return c

        lax.fori_loop(0, n_chunks, chunk, 0)

    def scatter(tok_ref, dest_ref, t, base):
        def row_copy(r, d):
            return pltpu.make_async_copy(tok_ref.at[pl.ds(r, 1)], xs_ref.at[pl.ds(d, 1)], sem)

        def issue(r, c):
            for k in range(TOP_K):
                row_copy(r, dest_ref[(base + r) * TOP_K + k]).start()
            return c

        lax.fori_loop(0, tm, issue, 0)

        def drain(r, c):
            for k in range(TOP_K):
                row_copy(r, 0).wait()
            return c

        lax.fori_loop(0, tm, drain, 0)

    @pl.when(i < n_a)
    def _():
        scatter(tok_a_ref, dest_a_ref, t_a, i * tm)

    @pl.when(i >= n_a)
    def _():
        scatter(tok_b_ref, dest_b_ref, t_b, (i - n_a) * tm)

    @pl.when(i == 0)
    def _():
        def pad_wait(r, c):
            pad_copy(0).wait()
            return c

        lax.fori_loop(0, meta_ref[0], pad_wait, 0)

        def chunk_wait(b, c):
            chunk_copy(0).wait()
            return c

        lax.fori_loop(0, meta_ref[1], chunk_wait, 0)


def _dispatch(dest_a, dest_b, pad_off, pad_len, used, meta, tok_a, tok_b, n_slots, tm):
    (t_a, d), t_b = tok_a.shape, tok_b.shape[0]
    assert t_a % tm == 0 and t_b % tm == 0 and n_slots % EXPERT_CHUNK == 0
    n_a, n_b = t_a // tm, t_b // tm
    n_e = pad_off.shape[0]
    return pl.pallas_call(
        functools.partial(_dispatch_kernel, tm=tm, t_a=t_a, t_b=t_b, n_e=n_e),
        out_shape=jax.ShapeDtypeStruct((n_slots, d), F32),
        grid_spec=pltpu.PrefetchScalarGridSpec(
            num_scalar_prefetch=6,
            grid=(n_a + n_b,),
            in_specs=[pl.BlockSpec((tm, d), lambda i, *_: (jnp.minimum(i, n_a - 1), 0)),
                      pl.BlockSpec((tm, d), lambda i, *_: (jnp.maximum(i - n_a, 0), 0))],
            out_specs=pl.BlockSpec(memory_space=pl.ANY),
            scratch_shapes=[pltpu.VMEM((EXPERT_CHUNK, d), F32), pltpu.SemaphoreType.DMA(()),
                            pltpu.SemaphoreType.DMA(()), pltpu.SemaphoreType.DMA(())],
        ),
        compiler_params=pltpu.CompilerParams(
            dimension_semantics=("arbitrary",), has_side_effects=True),
        name="dispatch",
    )(dest_a, dest_b, pad_off, pad_len, used, meta, tok_a, tok_b)


def _is_new_expert(be_ref, i):
    return jnp.logical_or(i == 0, be_ref[i] != be_ref[jnp.maximum(i - 1, 0)])


def _used_tile_rows(j, i, be, nv, *_):
    return (jnp.where(nv[i] > 0, i, 0), 0)


def _per_used_chunks(nv, rows, compute, zero_fill):
    n_chunks = rows // EXPERT_CHUNK
    for n in range(n_chunks + 1):
        @pl.when(nv == n)
        def _(n=n):
            if n:
                compute(n * EXPERT_CHUNK)
            if n < n_chunks:
                zero_fill(n * EXPERT_CHUNK)


def _stream_weights(be_ref, grp_ref, nxt_ref, sched_ref, w_hbm, wbuf, w_bf, sems):
    j, i = pl.program_id(0), pl.program_id(1)
    tn = w_bf[0].shape[1]

    def fetch(e, jj, slot):
        cols = pl.ds(pl.multiple_of(jj * tn, tn), tn)
        return [pltpu.make_async_copy(w.at[e, :, cols], wbuf.at[slot, k], sems.at[slot, k])
                for k, w in enumerate(w_hbm)]

    @pl.when(_is_new_expert(be_ref, i))
    def _():
        step = j * sched_ref[0] + grp_ref[i]
        slot = step % 2

        @pl.when(step == 0)
        def _():
            for c in fetch(be_ref[i], j, slot):
                c.start()

        for c in fetch(be_ref[i], j, slot):
            c.wait()
        nxt = nxt_ref[i]

        @pl.when(nxt >= 0)
        def _():
            for c in fetch(nxt, j, 1 - slot):
                c.start()

        @pl.when(jnp.logical_and(nxt < 0, j + 1 < pl.num_programs(0)))
        def _():
            for c in fetch(sched_ref[1], j + 1, 1 - slot):
                c.start()

        for k, dst in enumerate(w_bf):
            dst[...] = wbuf[slot, k].astype(BF16)


def _expert_up_kernel(be_ref, nv_ref, grp_ref, nxt_ref, sched_ref, xs_ref, wg_hbm, wu_hbm, bg_ref, bu_ref,
                      h_ref, wbuf, wg_bf, wu_bf, sems):
    i = pl.program_id(1)
    _stream_weights(be_ref, grp_ref, nxt_ref, sched_ref, (wg_hbm, wu_hbm), wbuf, (wg_bf, wu_bf), sems)

    def compute(m):
        x = xs_ref[0:m, :].astype(BF16)
        tf = h_ref.shape[1]
        cw = min(tf, 256)
        for c in range(tf // cw):
            cs = slice(c * cw, (c + 1) * cw)
            glu = jnp.dot(x, wg_bf[:, cs], preferred_element_type=F32) + bg_ref[0, :, cs]
            lin = jnp.dot(x, wu_bf[:, cs], preferred_element_type=F32) + bu_ref[0, :, cs]
            glu = jnp.minimum(glu, SWIGLU_LIMIT)
            lin = jnp.clip(lin, -SWIGLU_LIMIT, SWIGLU_LIMIT)
            h_ref[0:m, cs] = (glu * jax.nn.sigmoid(SWIGLU_ALPHA * glu) * (lin + 1.0)).astype(h_ref.dtype)

    def zero_fill(r0):
        h_ref[r0:, :] = jnp.zeros((h_ref.shape[0] - r0, h_ref.shape[1]), h_ref.dtype)

    _per_used_chunks(nv_ref[i], h_ref.shape[0], compute, zero_fill)


def _expert_up(sched, xs, wg, wu, bg, bu, tf):
    ns, d = xs.shape
    n_e, _, dff = wg.shape
    p = EXPERT_ROWS
    tf = min(tf, dff)
    assert ns % p == 0 and dff % tf == 0
    est = 2 * 2 * d * tf * 4 + 2 * d * tf * 2 + 2 * p * d * 4 + 2 * p * tf * 2 + 4 * p * tf * 4
    bspec = pl.BlockSpec((1, 1, tf), lambda j, i, be, *_: (be[i], 0, j))
    hbm = pl.BlockSpec(memory_space=pl.ANY)
    return pl.pallas_call(
        _expert_up_kernel,
        out_shape=jax.ShapeDtypeStruct((ns, dff), BF16),
        grid_spec=pltpu.PrefetchScalarGridSpec(
            num_scalar_prefetch=5,
            grid=(dff // tf, ns // p),
            in_specs=[pl.BlockSpec((p, d), _used_tile_rows), hbm, hbm, bspec, bspec],
            out_specs=pl.BlockSpec((p, tf), lambda j, i, *_: (i, j)),
            scratch_shapes=[pltpu.VMEM((2, 2, d, tf), F32), pltpu.VMEM((d, tf), BF16),
                            pltpu.VMEM((d, tf), BF16), pltpu.SemaphoreType.DMA((2, 2))],
        ),
        compiler_params=pltpu.CompilerParams(
            dimension_semantics=("arbitrary", "arbitrary"), vmem_limit_bytes=_vmem_limit(est)),
        name="expert_up",
    )(*sched, xs, wg, wu, bg.reshape(n_e, 1, dff), bu.reshape(n_e, 1, dff))


def _expert_down_kernel(be_ref, nv_ref, grp_ref, nxt_ref, sched_ref, h_ref, wd_hbm, bd_ref, y_ref,
                        wbuf, wd_bf, sems):
    i = pl.program_id(1)
    _stream_weights(be_ref, grp_ref, nxt_ref, sched_ref, (wd_hbm,), wbuf, (wd_bf,), sems)

    def compute(m):
        y_ref[0:m, :] = jnp.dot(h_ref[0:m, :], wd_bf[...], preferred_element_type=F32) + bd_ref[0]

    def zero_fill(r0):
        y_ref[r0:, :] = jnp.zeros((y_ref.shape[0] - r0, y_ref.shape[1]), y_ref.dtype)

    _per_used_chunks(nv_ref[i], y_ref.shape[0], compute, zero_fill)


def _expert_down(sched, hmid, wd, bd, tn):
    ns, dff = hmid.shape
    n_e, _, d = wd.shape
    p = EXPERT_ROWS
    tn = min(tn, d)
    assert ns % p == 0 and d % tn == 0
    est = 2 * dff * tn * 4 + dff * tn * 2 + 2 * p * dff * 2 + 2 * p * tn * 4 + 2 * p * tn * 4
    return pl.pallas_call(
        _expert_down_kernel,
        out_shape=jax.ShapeDtypeStruct((ns, d), F32),
        grid_spec=pltpu.PrefetchScalarGridSpec(
            num_scalar_prefetch=5,
            grid=(d // tn, ns // p),
            in_specs=[pl.BlockSpec((p, dff), _used_tile_rows),
                      pl.BlockSpec(memory_space=pl.ANY),
                      pl.BlockSpec((1, 1, tn), lambda j, i, be, *_: (be[i], 0, j))],
            out_specs=pl.BlockSpec((p, tn), lambda j, i, *_: (i, j)),
            scratch_shapes=[pltpu.VMEM((2, 1, dff, tn), F32), pltpu.VMEM((dff, tn), BF16),
                            pltpu.SemaphoreType.DMA((2, 1))],
        ),
        compiler_params=pltpu.CompilerParams(
            dimension_semantics=("arbitrary", "arbitrary"), vmem_limit_bytes=_vmem_limit(est)),
        name="expert_down",
    )(*sched, hmid, wd, bd.reshape(n_e, 1, d))


def _combine_kernel(dest_ref, h_ref, gate_ref, gain_ref, ys_ref, o_ref, buf_ref, sem, *, tm, t):
    base = pl.program_id(0) * tm

    def row_copy(r, k, d):
        return pltpu.make_async_copy(ys_ref.at[pl.ds(d, 1)], buf_ref.at[k, pl.ds(r, 1)], sem)

    def issue(r, c):
        for k in range(TOP_K):
            row_copy(r, k, dest_ref[(base + r) * TOP_K + k]).start()
        return c

    lax.fori_loop(0, tm, issue, 0)

    def drain(r, c):
        for k in range(TOP_K):
            row_copy(r, k, 0).wait()
        return c

    lax.fori_loop(0, tm, drain, 0)

    acc = h_ref[...]
    gates = gate_ref[...]
    for k in range(TOP_K):
        acc = acc + gates[:, k:k + 1] * buf_ref[k]
    o_ref[...] = _rms(acc, gain_ref[...])


def _combine(dest_flat, h, gates, gain, ys, tm):
    t, d = h.shape
    tm = min(tm, t)
    assert t % tm == 0
    est = TOP_K * tm * d * 4 + 4 * tm * d * 4 + 2 * tm * d * 4
    return pl.pallas_call(
        functools.partial(_combine_kernel, tm=tm, t=t),
        out_shape=jax.ShapeDtypeStruct((t, d), F32),
        grid_spec=pltpu.PrefetchScalarGridSpec(
            num_scalar_prefetch=1,
            grid=(t // tm,),
            in_specs=[pl.BlockSpec((tm, d), lambda i, dest: (i, 0)),
                      pl.BlockSpec((tm, TOP_K), lambda i, dest: (i, 0)),
                      pl.BlockSpec((1, d), lambda i, dest: (0, 0)),
                      pl.BlockSpec(memory_space=pl.ANY)],
            out_specs=pl.BlockSpec((tm, d), lambda i, dest: (i, 0)),
            scratch_shapes=[pltpu.VMEM((TOP_K, tm, d), F32), pltpu.SemaphoreType.DMA(())],
        ),
        compiler_params=pltpu.CompilerParams(
            dimension_semantics=("arbitrary",), vmem_limit_bytes=_vmem_limit(est)),
        name="combine",
    )(dest_flat, h, gates, gain, ys)


def kernel(x_prompt, x_sample, cache_k, cache_v, state_pool, norm_mix, w_in, attn_sinks, w_pool_mix,
           pool_scale, w_attn_out, w_pool_out, w_out, norm_ffn, w_router, b_router, w_gate, b_gate,
           w_up, b_up, w_down, b_down, norm_final):
    assert norm_mix.shape[0] == 1, "single-layer step"
    bp, sp, d = x_prompt.shape
    bs, ns, _ = x_sample.shape
    n_kv, hd = cache_k.shape[-2:]
    n_q = attn_sinks.shape[1]
    qw, kvw = n_q * hd, n_kv * hd
    pw = state_pool.shape[-1]
    n_e = w_router.shape[-1]
    tp_, ts_ = bp * sp, bs * ns
    t_all = tp_ + ts_

    gain_mix = norm_mix[0].reshape(1, d)
    w_in_bf = w_in[0].astype(BF16)
    w_q = w_in_bf[:, :qw]
    w_kvu = w_in_bf[:, qw:qw + 2 * kvw + pw]
    w_g = w_in_bf[:, qw + 2 * kvw + pw:]
    wmix = w_pool_mix[0].astype(BF16)
    pscale = pool_scale[0].reshape(1, pw)
    wa = w_attn_out[0].astype(BF16)
    wp = w_pool_out[0].astype(BF16)
    wo = w_out[0].astype(BF16)
    sinks = attn_sinks[0].astype(F32)

    def project(x2d):
        xn = _rms_cast(x2d, gain_mix, 512)
        q = _proj(xn, w_q, BF16, "scale", 1024, 1024, "proj_q")
        kvu = _proj(xn, w_kvu, F32, "none", 1024, 768, "proj_kvu")
        sg = _proj(xn, w_g, BF16, "sigmoid", 1024, 1024, "proj_gates")
        return q, kvu, sg

    xp = x_prompt.reshape(tp_, d)
    q_p, kvu_p, sg_p = project(xp)
    attn_p = _attn_prompt(q_p, kvu_p, sinks, sp, n_kv, hd)
    pool_p = _pool_prompt(kvu_p, 2 * kvw, pw, wmix, pscale, sp, 512)

    xs_ = x_sample.reshape(ts_, d)
    q_s, kvu_s, sg_s = project(xs_)
    k_s = kvu_s[:, :kvw].reshape(bs, ns, kvw)
    v_s = kvu_s[:, kvw:2 * kvw].reshape(bs, ns, kvw)
    u_s = kvu_s[:, 2 * kvw:].reshape(bs, ns, pw)
    ck = cache_k[0].reshape(bs, -1, kvw)
    cv = cache_v[0].reshape(bs, -1, kvw)
    o_s = _attn_sample(q_s.reshape(bs, ns, qw), k_s, v_s, ck, cv, sinks, n_kv, hd)
    attn_s = (o_s.reshape(bs, n_kv, n_q // n_kv, ns, hd).transpose(0, 3, 1, 2, 4)
              .reshape(ts_, qw).astype(BF16))
    ext = jnp.concatenate([state_pool[0], u_s], axis=1)
    pool_s = _pool_sample(ext.transpose(1, 0, 2), ns, wmix, pscale)
    pool_s = pool_s.transpose(1, 0, 2).reshape(ts_, pw)

    gain_ffn = norm_ffn[0].reshape(1, d)
    wr = w_router[0].T.astype(BF16)
    br = b_router[0].reshape(n_e, 1).astype(F32)
    tm = 256
    h_p, tok_p, idx_p, rank_p, gates_p, cnt_p = _merge_route(
        attn_p, pool_p, sg_p, xp, wa, wp, wo, gain_ffn, wr, br, jnp.zeros((n_e, 1), F32), tm)
    h_s, tok_s, idx_s, rank_s, gates_s, counts = _merge_route(
        attn_s, pool_s, sg_s, xs_, wa, wp, wo, gain_ffn, wr, br, cnt_p, tm)

    p = EXPERT_ROWS
    cnt = counts[:, 0].astype(I32)
    padded = (cnt + p - 1) // p * p
    pad_end = jnp.cumsum(padded)
    pad_start = pad_end - padded
    experts = jnp.arange(n_e, dtype=I32)[:, None, None]

    def slot_ids(idx, rank):
        start = jnp.sum(jnp.where(idx[None] == experts, pad_start[:, None, None], 0), axis=0)
        return (start + rank).T.reshape(-1)

    dest_p = slot_ids(idx_p, rank_p)
    dest_s = slot_ids(idx_s, rank_s)
    n_slots = -(-(t_all * TOP_K) // p) * p + n_e * p
    n_tiles = n_slots // p
    tile_start = jnp.arange(n_tiles, dtype=I32) * p
    blk_ok = (tile_start < pad_end[-1]).astype(I32)
    blk_e = jnp.sum((pad_end[None, :] <= tile_start[:, None]).astype(I32), axis=1)
    blk_e = jnp.minimum(blk_e, n_e - 1)
    blk_e = jnp.where(blk_ok == 1, blk_e, jnp.max(blk_e * blk_ok))
    ck = EXPERT_CHUNK
    chunk_start = jnp.arange(n_slots // ck, dtype=I32)[:, None] * ck
    used = jnp.any((pad_start[None, :] <= chunk_start) & (chunk_start < (pad_start + cnt)[None, :]),
                   axis=1).astype(I32)
    blk_nv = jnp.sum(used.reshape(n_tiles, p // ck), axis=1)
    pad_len = (cnt + ck - 1) // ck * ck - cnt

    meta = jnp.stack([jnp.sum(pad_len), used.shape[0] - jnp.sum(used)]).astype(I32)
    xs_sorted = _dispatch(dest_p, dest_s, pad_start + cnt, pad_len, used, meta, tok_p, tok_s, n_slots, tm)
    new_grp = jnp.concatenate([jnp.ones((1,), I32), (blk_e[1:] != blk_e[:-1]).astype(I32)])
    blk_grp = jnp.cumsum(new_grp) - 1
    n_groups = blk_grp[-1] + 1
    eids = jnp.arange(n_e, dtype=I32)
    order = jnp.cumsum((cnt > 0).astype(I32)) - 1
    grp_e = jnp.sum(jnp.where((cnt > 0)[None, :] & (order[None, :] == eids[:, None]), eids[None, :], 0), axis=1)
    blk_nxt = jnp.sum(jnp.where(eids[None, :] == (blk_grp + 1)[:, None], grp_e[None, :], 0), axis=1)
    blk_nxt = jnp.where(blk_grp + 1 < n_groups, blk_nxt, -1)
    sched = (blk_e, blk_nv, blk_grp, blk_nxt, jnp.stack([n_groups, grp_e[0]]))
    hmid = _expert_up(sched, xs_sorted, w_gate[0], w_up[0], b_gate[0], b_up[0], 1024)
    ys = _expert_down(sched, hmid, w_down[0], b_down[0], 1024)

    gain_fin = norm_final.reshape(1, d)
    y_p = _combine(dest_p, h_p, gates_p.T, gain_fin, ys, tm)
    y_s = _combine(dest_s, h_s, gates_s.T, gain_fin, ys, tm)

    keep = min(WINDOW, sp)
    tail_p = kvu_p.reshape(bp, sp, 2 * kvw + pw)[:, sp - keep:]
    n_rows = cache_k.shape[2]
    new_k_s = jnp.concatenate([cache_k[0], k_s.reshape(bs, ns, n_kv, hd)], axis=1)[:, -n_rows:]
    new_v_s = jnp.concatenate([cache_v[0], v_s.reshape(bs, ns, n_kv, hd)], axis=1)[:, -n_rows:]
    return (y_p.reshape(bp, sp, d), y_s.reshape(bs, ns, d),
            tail_p[:, :, :kvw].reshape(1, bp, keep, n_kv, hd),
            tail_p[:, :, kvw:2 * kvw].reshape(1, bp, keep, n_kv, hd),
            tail_p[None, :, keep - POOL_STATE:, 2 * kvw:],
            new_k_s[None], new_v_s[None], ext[None, :, -POOL_STATE:])
```

```python
import functools

import jax
import jax.numpy as jnp
import numpy as np
from jax import lax
from jax.experimental import pallas as pl
from jax.experimental.pallas import tpu as pltpu

F32 = jnp.float32
BF16 = jnp.bfloat16
I32 = jnp.int32

WINDOW = 128
POOL_WINDOWS = (2, 4, 8, 16)
POOL_STATE = max(POOL_WINDOWS) - 1
PAST_LEN = 16384
TOP_K = 4
SWIGLU_LIMIT = 7.0
SWIGLU_ALPHA = 1.702
RMS_EPS = 1e-5
NEG_BIG = -1e30
LOG2E = 1.4426950408889634

V7X_VMEM_BYTES = 64 * 1024 * 1024
EXPERT_ROWS = 512
EXPERT_CHUNK = 256
HALO = 16


def _vmem_limit(nbytes):
    return int(min(nbytes + (8 << 20), V7X_VMEM_BYTES - (4 << 20)))


def _rms(x, gain):
    ms = jnp.mean(x * x, axis=-1, keepdims=True)
    return x * lax.rsqrt(ms + RMS_EPS) * gain


def _rms_cast_kernel(x_ref, g_ref, o_ref):
    o_ref[...] = _rms(x_ref[...], g_ref[...]).astype(o_ref.dtype)


def _rms_cast(x, gain, tm):
    t, d = x.shape
    tm = min(tm, t)
    assert t % tm == 0
    return pl.pallas_call(
        _rms_cast_kernel,
        out_shape=jax.ShapeDtypeStruct((t, d), BF16),
        grid=(t // tm,),
        in_specs=[pl.BlockSpec((tm, d), lambda i: (i, 0)), pl.BlockSpec((1, d), lambda i: (0, 0))],
        out_specs=pl.BlockSpec((tm, d), lambda i: (i, 0)),
        compiler_params=pltpu.CompilerParams(
            dimension_semantics=("parallel",), vmem_limit_bytes=_vmem_limit(16 * tm * d)),
        name="rms_cast",
    )(x, gain)


def _proj_kernel(xn_ref, w_ref, o_ref, *, epilogue):
    acc = jnp.dot(xn_ref[...], w_ref[...], preferred_element_type=F32)
    if epilogue == "sigmoid":
        acc = jax.nn.sigmoid(acc)
    elif epilogue == "scale":
        acc = acc * 0.125
    o_ref[...] = acc.astype(o_ref.dtype)


def _proj(xn, w, out_dtype, epilogue, tm, tn, name):
    t, d = xn.shape
    n = w.shape[1]
    tm = min(tm, t)
    tn = min(tn, n)
    assert t % tm == 0 and n % tn == 0
    est = 2 * tm * d * 2 + 2 * d * tn * 2 + 2 * tm * tn * 4 + 2 * tm * tn * 4
    return pl.pallas_call(
        functools.partial(_proj_kernel, epilogue=epilogue),
        out_shape=jax.ShapeDtypeStruct((t, n), out_dtype),
        grid=(n // tn, t // tm),
        in_specs=[
            pl.BlockSpec((tm, d), lambda j, i: (i, 0)),
            pl.BlockSpec((d, tn), lambda j, i: (0, j)),
        ],
        out_specs=pl.BlockSpec((tm, tn), lambda j, i: (i, j)),
        compiler_params=pltpu.CompilerParams(
            dimension_semantics=("parallel", "parallel"),
            vmem_limit_bytes=_vmem_limit(est)),
        name=name,
    )(xn, w)


def _attn_prompt_kernel(sink_ref, bias_ref, q_ref, kc_ref, vc_ref, kp_ref, vp_ref, o_ref, *,
                        n_kv, q_per_kv, hd, blocks_per_seq):
    blk = WINDOW
    rq = q_per_kv
    nt = (((1,), (1,)), ((), ()))
    has_prev = (pl.program_id(0) % blocks_per_seq) > 0
    kj = lax.broadcasted_iota(I32, (blk, 2 * blk), 1)
    prev_ok = (kj >= blk) | has_prev

    def scores(g):
        cs = slice(g * hd, (g + 1) * hd)
        k = (jnp.concatenate([kp_ref[:, cs], kc_ref[:, cs]], axis=0) * LOG2E).astype(BF16)
        qs = jnp.concatenate([q_ref[:, (g * rq + r) * hd:(g * rq + r + 1) * hd] for r in range(rq)], axis=0)
        s = lax.dot_general(qs, k, nt, preferred_element_type=F32)
        slabs, maxes = [], []
        for r in range(rq):
            h = g * rq + r
            sr = jnp.where(prev_ok, s[r * blk:(r + 1) * blk] + bias_ref[h], NEG_BIG)
            slabs.append(sr)
            maxes.append(jnp.maximum(jnp.max(sr, axis=-1, keepdims=True), sink_ref[h] * LOG2E))
        return slabs, maxes

    def finish(g, slabs, maxes):
        cs = slice(g * hd, (g + 1) * hd)
        v = jnp.concatenate([vp_ref[:, cs], vc_ref[:, cs]], axis=0).astype(BF16)
        probs, inv = [], []
        for r in range(rq):
            p = jnp.exp2(slabs[r] - maxes[r])
            denom = jnp.sum(p, axis=-1, keepdims=True) + jnp.exp2(sink_ref[g * rq + r] * LOG2E - maxes[r])
            probs.append(p.astype(BF16))
            inv.append(1.0 / denom)
        o = jnp.dot(jnp.concatenate(probs, axis=0), v, preferred_element_type=F32)
        for r in range(rq):
            h = g * rq + r
            o_ref[:, h * hd:(h + 1) * hd] = (o[r * blk:(r + 1) * blk] * inv[r]).astype(o_ref.dtype)

    pending = scores(0)
    for g in range(n_kv):
        nxt = scores(g + 1) if g + 1 < n_kv else None
        finish(g, *pending)
        pending = nxt


def _attn_prompt(q, kvu, sinks, seq, n_kv, hd):
    t, qw = q.shape
    n_q = qw // hd
    kvw = n_kv * hd
    blk = WINDOW
    assert t % blk == 0 and seq % blk == 0
    slopes = 2.0 ** (-8.0 * np.arange(1, n_q + 1, dtype=np.float64) / n_q)
    dist = (np.arange(blk)[:, None] + blk - np.arange(2 * blk)[None, :]).astype(np.float64)
    band = (dist >= 0) & (dist <= WINDOW)
    bias = np.where(band[None], -LOG2E * slopes[:, None, None] * dist[None], NEG_BIG).astype(np.float32)
    prev = lambda i: jnp.maximum(i - 1, 0)
    return pl.pallas_call(
        functools.partial(_attn_prompt_kernel, n_kv=n_kv, q_per_kv=n_q // n_kv, hd=hd,
                          blocks_per_seq=seq // blk),
        out_shape=jax.ShapeDtypeStruct((t, qw), BF16),
        grid=(t // blk,),
        in_specs=[
            pl.BlockSpec(memory_space=pltpu.SMEM),
            pl.BlockSpec((n_q, blk, 2 * blk), lambda i: (0, 0, 0)),
            pl.BlockSpec((blk, qw), lambda i: (i, 0)),
            pl.BlockSpec((blk, kvw), lambda i: (i, 0)),
            pl.BlockSpec((blk, kvw), lambda i: (i, 1)),
            pl.BlockSpec((blk, kvw), lambda i: (prev(i), 0)),
            pl.BlockSpec((blk, kvw), lambda i: (prev(i), 1)),
        ],
        out_specs=pl.BlockSpec((blk, qw), lambda i: (i, 0)),
        compiler_params=pltpu.CompilerParams(dimension_semantics=("parallel",)),
        name="attn_prompt",
    )(sinks, jnp.asarray(bias), q, kvu, kvu, kvu, kvu)


def _attn_sample_kernel(qbd_ref, ck_ref, cv_ref, kn_ref, vn_ref, bc_ref, bn_ref, sink_ref, o_ref, *,
                        bb, n_kv, hd):
    nt = (((1,), (1,)), ((), ()))
    rows = qbd_ref.shape[1]
    grp = lax.broadcasted_iota(I32, (rows, 1), 0) // (rows // n_kv)
    sink = sink_ref[...]
    for b in range(bb):
        qb = qbd_ref[b]
        s_c = lax.dot_general(qb, ck_ref[b].astype(BF16), nt, preferred_element_type=F32) + bc_ref[...]
        s_n = lax.dot_general(qb, kn_ref[b].astype(BF16), nt, preferred_element_type=F32) + bn_ref[...]
        m = jnp.maximum(jnp.maximum(jnp.max(s_c, axis=-1, keepdims=True),
                                    jnp.max(s_n, axis=-1, keepdims=True)), sink)
        p_c = jnp.exp(s_c - m)
        p_n = jnp.exp(s_n - m)
        denom = (jnp.sum(p_c, axis=-1, keepdims=True) + jnp.sum(p_n, axis=-1, keepdims=True)
                 + jnp.exp(sink - m))
        o = (jnp.dot(p_c.astype(BF16), cv_ref[b].astype(BF16), preferred_element_type=F32)
             + jnp.dot(p_n.astype(BF16), vn_ref[b].astype(BF16), preferred_element_type=F32))
        sel = jnp.zeros((rows, hd), F32)
        for g in range(n_kv):
            sel = sel + jnp.where(grp == g, o[:, g * hd:(g + 1) * hd], 0.0)
        o_ref[b] = sel / denom


def _attn_sample(q_s, k_new, v_new, cache_k, cache_v, sinks, n_kv, hd):
    bd, n, qw = q_s.shape
    n_q = qw // hd
    r = n_q // n_kv
    kvw = n_kv * hd
    w = cache_k.shape[1]
    rows = n_q * n
    npad = 16
    q5 = q_s.reshape(bd, n, n_kv, r, hd)
    qbd = jnp.einsum("bigrd,gh->bgrihd", q5, jnp.eye(n_kv, dtype=q_s.dtype)).reshape(bd, rows, kvw)
    pad = ((0, 0), (0, npad - n), (0, 0))
    k_new = jnp.pad(k_new, pad)
    v_new = jnp.pad(v_new, pad)
    slopes = 2.0 ** (-8.0 * np.arange(1, n_q + 1, dtype=np.float64) / n_q)
    slope_c = np.repeat(slopes, n)
    i_c = np.tile(np.arange(n), n_q)
    dist_c = (w + i_c)[:, None] - np.arange(w)[None, :]
    bias_c = np.where((dist_c >= 0) & (dist_c <= WINDOW), -slope_c[:, None] * dist_c, NEG_BIG)
    dist_n = i_c[:, None] - np.arange(npad)[None, :]
    ok_n = (dist_n >= 0) & (np.arange(npad)[None, :] < n)
    bias_n = np.where(ok_n, -slope_c[:, None] * dist_n, NEG_BIG)
    sink_c = jnp.repeat(sinks.astype(F32), n).reshape(rows, 1)
    bb = 8 if bd % 8 == 0 else 1
    return pl.pallas_call(
        functools.partial(_attn_sample_kernel, bb=bb, n_kv=n_kv, hd=hd),
        out_shape=jax.ShapeDtypeStruct((bd, rows, hd), F32),
        grid=(bd // bb,),
        in_specs=[
            pl.BlockSpec((bb, rows, kvw), lambda i: (i, 0, 0)),
            pl.BlockSpec((bb, w, kvw), lambda i: (i, 0, 0)),
            pl.BlockSpec((bb, w, kvw), lambda i: (i, 0, 0)),
            pl.BlockSpec((bb, npad, kvw), lambda i: (i, 0, 0)),
            pl.BlockSpec((bb, npad, kvw), lambda i: (i, 0, 0)),
            pl.BlockSpec((rows, w), lambda i: (0, 0)),
            pl.BlockSpec((rows, npad), lambda i: (0, 0)),
            pl.BlockSpec((rows, 1), lambda i: (0, 0)),
        ],
        out_specs=pl.BlockSpec((bb, rows, hd), lambda i: (i, 0, 0)),
        compiler_params=pltpu.CompilerParams(dimension_semantics=("parallel",)),
        name="attn_sample",
    )(qbd, cache_k, cache_v, k_new, v_new,
      jnp.asarray(bias_c, F32), jnp.asarray(bias_n, F32), sink_c)


def _pool_prompt_kernel(*refs, tp, gw, tiles_per_seq):
    ng = len(POOL_WINDOWS)
    cur = refs[:ng]
    halo = refs[ng:2 * ng]
    wmix_ref, scale_ref, o_ref, ext_ref = refs[2 * ng:]
    tile_in_seq = pl.program_id(0) % tiles_per_seq
    first = tile_in_seq == 0
    pos = lax.broadcasted_iota(I32, (tp, 1), 0) + tile_in_seq * tp
    for g, w in enumerate(POOL_WINDOWS):
        u = cur[g][...]
        ext_ref[0:HALO, :] = jnp.where(first, 0.0, halo[g][...])
        ext_ref[HALO:HALO + tp, :] = u
        acc = u
        for d in range(1, w):
            acc = acc + ext_ref[HALO - d:HALO - d + tp, :]
        cnt = jnp.minimum(w, pos + 1).astype(F32)
        pooled = acc / cnt - u
        z = jnp.dot(pooled.astype(BF16), wmix_ref[g], preferred_element_type=F32)
        o_ref[:, g * gw:(g + 1) * gw] = (z * scale_ref[:, g * gw:(g + 1) * gw]).astype(o_ref.dtype)


def _pool_prompt(kvu, u_col0, pw, wmix, scale, seq, tp):
    t = kvu.shape[0]
    ng = len(POOL_WINDOWS)
    gw = pw // ng
    tp = min(tp, seq)
    assert seq % tp == 0 and tp % HALO == 0 and u_col0 % gw == 0
    c0 = u_col0 // gw
    hb = tp // HALO
    cur_specs = [pl.BlockSpec((tp, gw), functools.partial(lambda i, g: (i, c0 + g), g=g)) for g in range(ng)]
    halo_specs = [pl.BlockSpec((HALO, gw),
                               functools.partial(lambda i, g: (jnp.maximum(i * hb - 1, 0), c0 + g), g=g))
                  for g in range(ng)]
    return pl.pallas_call(
        functools.partial(_pool_prompt_kernel, tp=tp, gw=gw, tiles_per_seq=seq // tp),
        out_shape=jax.ShapeDtypeStruct((t, pw), BF16),
        grid=(t // tp,),
        in_specs=cur_specs + halo_specs + [
            pl.BlockSpec((ng, gw, gw), lambda i: (0, 0, 0)),
            pl.BlockSpec((1, pw), lambda i: (0, 0)),
        ],
        out_specs=pl.BlockSpec((tp, pw), lambda i: (i, 0)),
        scratch_shapes=[pltpu.VMEM((HALO + tp, gw), F32)],
        compiler_params=pltpu.CompilerParams(dimension_semantics=("parallel",)),
        name="pool_prompt",
    )(*([kvu] * (2 * ng)), wmix, scale)


def _pool_sample_kernel(ext_ref, wmix_ref, scale_ref, o_ref, *, n_new, gw):
    n_prev = ext_ref.shape[0] - n_new
    for i in range(n_new):
        for g, w in enumerate(POOL_WINDOWS):
            cs = slice(g * gw, (g + 1) * gw)
            u = ext_ref[n_prev + i, :, cs]
            acc = u
            for d in range(1, w):
                acc = acc + ext_ref[n_prev + i - d, :, cs]
            cnt = float(min(w, PAST_LEN + i + 1))
            pooled = acc / cnt - u
            z = jnp.dot(pooled.astype(BF16), wmix_ref[g], preferred_element_type=F32)
            o_ref[i, :, cs] = (z * scale_ref[:, cs]).astype(o_ref.dtype)


def _pool_sample(ext_t, n_new, wmix, scale):
    rows, bd, pw = ext_t.shape
    gw = pw // len(POOL_WINDOWS)
    return pl.pallas_call(
        functools.partial(_pool_sample_kernel, n_new=n_new, gw=gw),
        out_shape=jax.ShapeDtypeStruct((n_new, bd, pw), BF16),
        compiler_params=pltpu.CompilerParams(vmem_limit_bytes=_vmem_limit(2 * rows * bd * pw * 4)),
        name="pool_sample",
    )(ext_t, wmix, scale)


def _merge_route_kernel(attn_ref, pool_ref, sga_ref, sgp_ref, x_ref, wa_ref, wp_ref, wo_ref,
                        gain_ref, wr_ref, br_ref, cnt_in_ref,
                        h_ref, tok_ref, idx_ref, rank_ref, gate_ref, cnt_out_ref, cnt_ref):
    i = pl.program_id(0)
    tm = x_ref.shape[0]
    n_e = wr_ref.shape[0]

    @pl.when(i == 0)
    def _():
        cnt_ref[...] = cnt_in_ref[...]

    a = jnp.dot(attn_ref[...], wa_ref[...], preferred_element_type=F32)
    p = jnp.dot(pool_ref[...], wp_ref[...], preferred_element_type=F32)
    mixed = sga_ref[...].astype(F32) * a + sgp_ref[...].astype(F32) * p
    h = x_ref[...] + jnp.dot(mixed.astype(BF16), wo_ref[...], preferred_element_type=F32)
    h_ref[...] = h
    tok = _rms(h, gain_ref[...])
    tok_ref[...] = tok

    logits = lax.dot_general(wr_ref[...], tok.astype(BF16), (((1,), (1,)), ((), ())),
                             preferred_element_type=F32) + br_ref[...]
    eid = lax.broadcasted_iota(I32, (n_e, tm), 0).astype(F32)
    work = logits
    member = jnp.zeros((n_e, tm), F32)
    vals, idxs = [], []
    for _ in range(TOP_K):
        mk = jnp.max(work, axis=0, keepdims=True)
        ik = jnp.min(jnp.where(work == mk, eid, float(n_e)), axis=0, keepdims=True)
        sel = eid == ik
        vals.append(mk)
        idxs.append(ik)
        member = member + sel.astype(F32)
        work = jnp.where(sel, -jnp.inf, work)
    ex = [jnp.exp(v - vals[0]) for v in vals]
    den = ex[0]
    for e in ex[1:]:
        den = den + e
    rr = lax.broadcasted_iota(I32, (tm, tm), 0)
    cc = lax.broadcasted_iota(I32, (tm, tm), 1)
    earlier = (rr < cc).astype(BF16)
    before = jnp.dot(member.astype(BF16), earlier, preferred_element_type=F32) + cnt_ref[...]
    ranks = [jnp.sum(jnp.where(eid == ik, before, 0.0), axis=0, keepdims=True) for ik in idxs]
    cnt_ref[...] = cnt_ref[...] + jnp.sum(member, axis=1, keepdims=True)

    krow = lax.broadcasted_iota(I32, (TOP_K, tm), 0)

    def rows(parts):
        out = jnp.zeros((TOP_K, tm), F32)
        for k, c in enumerate(parts):
            out = jnp.where(krow == k, c, out)
        return out

    idx_ref[...] = rows(idxs).astype(I32)
    rank_ref[...] = rows(ranks).astype(I32)
    gate_ref[...] = rows([e / den for e in ex])
    cnt_out_ref[...] = cnt_ref[...]


def _merge_route(attn, pool_o, sg, x, wa, wp, wo, gain, wr, br, cnt_in, tm):
    t, d = x.shape
    qw, pw, n_e = attn.shape[1], pool_o.shape[1], wr.shape[0]
    tm = min(tm, t)
    assert t % tm == 0
    per_tok = pl.BlockSpec((TOP_K, tm), lambda i: (0, i))
    row = lambda w: pl.BlockSpec((tm, w), lambda i: (i, 0))
    whole = lambda a: pl.BlockSpec(a.shape, lambda i: (0,) * a.ndim, pipeline_mode=pl.Buffered(1))
    out_shape = [
        jax.ShapeDtypeStruct((t, d), F32),
        jax.ShapeDtypeStruct((t, d), F32),
        jax.ShapeDtypeStruct((TOP_K, t), I32),
        jax.ShapeDtypeStruct((TOP_K, t), I32),
        jax.ShapeDtypeStruct((TOP_K, t), F32),
        jax.ShapeDtypeStruct((n_e, 1), F32),
    ]
    in_specs = [row(qw), row(pw), pl.BlockSpec((tm, d), lambda i: (i, 0)),
                pl.BlockSpec((tm, d), lambda i: (i, 1)), row(d),
                whole(wa), whole(wp), whole(wo), whole(gain), whole(wr), whole(br), whole(cnt_in)]
    est = (wa.size + wp.size + wo.size) * 2 + 2 * tm * (qw + pw + 2 * d) * 2 + 6 * tm * d * 4 + 8 * tm * d * 4
    return pl.pallas_call(
        _merge_route_kernel,
        out_shape=out_shape,
        grid=(t // tm,),
        in_specs=in_specs,
        out_specs=[row(d), row(d), per_tok, per_tok, per_tok,
                   pl.BlockSpec((n_e, 1), lambda i: (0, 0))],
        scratch_shapes=[pltpu.VMEM((n_e, 1), F32)],
        compiler_params=pltpu.CompilerParams(
            dimension_semantics=("arbitrary",), vmem_limit_bytes=_vmem_limit(est)),
        name="merge_route",
    )(attn, pool_o, sg, sg, x, wa, wp, wo, gain, wr, br, cnt_in)


def _dispatch_kernel(dest_a_ref, dest_b_ref, pad_off_ref, pad_len_ref, used_ref, meta_ref,
                     tok_a_ref, tok_b_ref, xs_ref, zero_ref, sem, pad_sem, chunk_sem, *,
                     tm, t_a, t_b, n_e):
    i = pl.program_id(0)
    n_a = t_a // tm
    p = zero_ref.shape[0]
    n_chunks = xs_ref.shape[0] // p

    def pad_copy(d):
        return pltpu.make_async_copy(zero_ref.at[pl.ds(0, 1)], xs_ref.at[pl.ds(d, 1)], pad_sem)

    def chunk_copy(b):
        return pltpu.make_async_copy(zero_ref, xs_ref.at[pl.ds(pl.multiple_of(b * p, p), p)], chunk_sem)

    @pl.when(i == 0)
    def _():
        zero_ref[...] = jnp.zeros_like(zero_ref)

        def pad_group(e, c):
            off = pad_off_ref[e]

            def one(r, c2):
                pad_copy(off + r).start()
                return c2

            lax.fori_loop(0, pad_len_ref[e], one, 0)
            return c

        lax.fori_loop(0, n_e, pad_group, 0)

        def chunk(b, c):
            @pl.when(used_ref[b] == 0)
            def _():
                chunk_copy(b).start()

            return c

        lax.fori_loop(0, n_chunks, chunk, 0)

    def scatter(tok_ref, dest_ref, t, base):
        def row_copy(r, d):
            return pltpu.make_async_copy(tok_ref.at[pl.ds(r, 1)], xs_ref.at[pl.ds(d, 1)], sem)

        def issue(r, c):
            for k in range(TOP_K):
                row_copy(r, dest_ref[(base + r) * TOP_K + k]).start()
            return c

        lax.fori_loop(0, tm, issue, 0)

        def drain(r, c):
            for k in range(TOP_K):
                row_copy(r, 0).wait()
            return c

        lax.fori_loop(0, tm, drain, 0)

    @pl.when(i < n_a)
    def _():
        scatter(tok_a_ref, dest_a_ref, t_a, i * tm)

    @pl.when(i >= n_a)
    def _():
        scatter(tok_b_ref, dest_b_ref, t_b, (i - n_a) * tm)

    @pl.when(i == 0)
    def _():
        def pad_wait(r, c):
            pad_copy(0).wait()
            return c

        lax.fori_loop(0, meta_ref[0], pad_wait, 0)

        def chunk_wait(b, c):
            chunk_copy(0).wait()
            return c

        lax.fori_loop(0, meta_ref[1], chunk_wait, 0)


def _dispatch(dest_a, dest_b, pad_off, pad_len, used, meta, tok_a, tok_b, n_slots, tm):
    (t_a, d), t_b = tok_a.shape, tok_b.shape[0]
    assert t_a % tm == 0 and t_b % tm == 0 and n_slots % EXPERT_CHUNK == 0
    n_a, n_b = t_a // tm, t_b // tm
    n_e = pad_off.shape[0]
    return pl.pallas_call(
        functools.partial(_dispatch_kernel, tm=tm, t_a=t_a, t_b=t_b, n_e=n_e),
        out_shape=jax.ShapeDtypeStruct((n_slots, d), F32),
        grid_spec=pltpu.PrefetchScalarGridSpec(
            num_scalar_prefetch=6,
            grid=(n_a + n_b,),
            in_specs=[pl.BlockSpec((tm, d), lambda i, *_: (jnp.minimum(i, n_a - 1), 0)),
                      pl.BlockSpec((tm, d), lambda i, *_: (jnp.maximum(i - n_a, 0), 0))],
            out_specs=pl.BlockSpec(memory_space=pl.ANY),
            scratch_shapes=[pltpu.VMEM((EXPERT_CHUNK, d), F32), pltpu.SemaphoreType.DMA(()),
                            pltpu.SemaphoreType.DMA(()), pltpu.SemaphoreType.DMA(())],
        ),
        compiler_params=pltpu.CompilerParams(
            dimension_semantics=("arbitrary",), has_side_effects=True),
        name="dispatch",
    )(dest_a, dest_b, pad_off, pad_len, used, meta, tok_a, tok_b)


def _is_new_expert(be_ref, i):
    return jnp.logical_or(i == 0, be_ref[i] != be_ref[jnp.maximum(i - 1, 0)])


def _used_tile_rows(j, i, be, nv, *_):
    return (jnp.where(nv[i] > 0, i, 0), 0)


def _per_used_chunks(nv, rows, compute, zero_fill):
    n_chunks = rows // EXPERT_CHUNK
    for n in range(n_chunks + 1):
        @pl.when(nv == n)
        def _(n=n):
            if n:
                compute(n * EXPERT_CHUNK)
            if n < n_chunks:
                zero_fill(n * EXPERT_CHUNK)


def _stream_weights(be_ref, grp_ref, nxt_ref, sched_ref, w_hbm, wbuf, w_bf, sems):
    j, i = pl.program_id(0), pl.program_id(1)
    tn = w_bf[0].shape[1]

    def fetch(e, jj, slot):
        cols = pl.ds(pl.multiple_of(jj * tn, tn), tn)
        return [pltpu.make_async_copy(w.at[e, :, cols], wbuf.at[slot, k], sems.at[slot, k])
                for k, w in enumerate(w_hbm)]

    @pl.when(_is_new_expert(be_ref, i))
    def _():
        step = j * sched_ref[0] + grp_ref[i]
        slot = step % 2

        @pl.when(step == 0)
        def _():
            for c in fetch(be_ref[i], j, slot):
                c.start()

        for c in fetch(be_ref[i], j, slot):
            c.wait()
        nxt = nxt_ref[i]

        @pl.when(nxt >= 0)
        def _():
            for c in fetch(nxt, j, 1 - slot):
                c.start()

        @pl.when(jnp.logical_and(nxt < 0, j + 1 < pl.num_programs(0)))
        def _():
            for c in fetch(sched_ref[1], j + 1, 1 - slot):
                c.start()

        for k, dst in enumerate(w_bf):
            dst[...] = wbuf[slot, k].astype(BF16)


def _expert_up_kernel(be_ref, nv_ref, grp_ref, nxt_ref, sched_ref, xs_ref, wg_hbm, wu_hbm, bg_ref, bu_ref,
                      h_ref, wbuf, wg_bf, wu_bf, sems):
    i = pl.program_id(1)
    _stream_weights(be_ref, grp_ref, nxt_ref, sched_ref, (wg_hbm, wu_hbm), wbuf, (wg_bf, wu_bf), sems)

    def compute(m):
        x = xs_ref[0:m, :].astype(BF16)
        tf = h_ref.shape[1]
        cw = min(tf, 256)
        for c in range(tf // cw):
            cs = slice(c * cw, (c + 1) * cw)
            glu = jnp.dot(x, wg_bf[:, cs], preferred_element_type=F32) + bg_ref[0, :, cs]
            lin = jnp.dot(x, wu_bf[:, cs], preferred_element_type=F32) + bu_ref[0, :, cs]
            glu = jnp.minimum(glu, SWIGLU_LIMIT)
            lin = jnp.clip(lin, -SWIGLU_LIMIT, SWIGLU_LIMIT)
            h_ref[0:m, cs] = (glu * jax.nn.sigmoid(SWIGLU_ALPHA * glu) * (lin + 1.0)).astype(h_ref.dtype)

    def zero_fill(r0):
        h_ref[r0:, :] = jnp.zeros((h_ref.shape[0] - r0, h_ref.shape[1]), h_ref.dtype)

    _per_used_chunks(nv_ref[i], h_ref.shape[0], compute, zero_fill)


def _expert_up(sched, xs, wg, wu, bg, bu, tf):
    ns, d = xs.shape
    n_e, _, dff = wg.shape
    p = EXPERT_ROWS
    tf = min(tf, dff)
    assert ns % p == 0 and dff % tf == 0
    est = 2 * 2 * d * tf * 4 + 2 * d * tf * 2 + 2 * p * d * 4 + 2 * p * tf * 2 + 4 * p * tf * 4
    bspec = pl.BlockSpec((1, 1, tf), lambda j, i, be, *_: (be[i], 0, j))
    hbm = pl.BlockSpec(memory_space=pl.ANY)
    return pl.pallas_call(
        _expert_up_kernel,
        out_shape=jax.ShapeDtypeStruct((ns, dff), BF16),
        grid_spec=pltpu.PrefetchScalarGridSpec(
            num_scalar_prefetch=5,
            grid=(dff // tf, ns // p),
            in_specs=[pl.BlockSpec((p, d), _used_tile_rows), hbm, hbm, bspec, bspec],
            out_specs=pl.BlockSpec((p, tf), lambda j, i, *_: (i, j)),
            scratch_shapes=[pltpu.VMEM((2, 2, d, tf), F32), pltpu.VMEM((d, tf), BF16),
                            pltpu.VMEM((d, tf), BF16), pltpu.SemaphoreType.DMA((2, 2))],
        ),
        compiler_params=pltpu.CompilerParams(
            dimension_semantics=("arbitrary", "arbitrary"), vmem_limit_bytes=_vmem_limit(est)),
        name="expert_up",
    )(*sched, xs, wg, wu, bg.reshape(n_e, 1, dff), bu.reshape(n_e, 1, dff))


def _expert_down_kernel(be_ref, nv_ref, grp_ref, nxt_ref, sched_ref, h_ref, wd_hbm, bd_ref, y_ref,
                        wbuf, wd_bf, sems):
    i = pl.program_id(1)
    _stream_weights(be_ref, grp_ref, nxt_ref, sched_ref, (wd_hbm,), wbuf, (wd_bf,), sems)

    def compute(m):
        y_ref[0:m, :] = jnp.dot(h_ref[0:m, :], wd_bf[...], preferred_element_type=F32) + bd_ref[0]

    def zero_fill(r0):
        y_ref[r0:, :] = jnp.zeros((y_ref.shape[0] - r0, y_ref.shape[1]), y_ref.dtype)

    _per_used_chunks(nv_ref[i], y_ref.shape[0], compute, zero_fill)


def _expert_down(sched, hmid, wd, bd, tn):
    ns, dff = hmid.shape
    n_e, _, d = wd.shape
    p = EXPERT_ROWS
    tn = min(tn, d)
    assert ns % p == 0 and d % tn == 0
    est = 2 * dff * tn * 4 + dff * tn * 2 + 2 * p * dff * 2 + 2 * p * tn * 4 + 2 * p * tn * 4
    return pl.pallas_call(
        _expert_down_kernel,
        out_shape=jax.ShapeDtypeStruct((ns, d), F32),
        grid_spec=pltpu.PrefetchScalarGridSpec(
            num_scalar_prefetch=5,
            grid=(d // tn, ns // p),
            in_specs=[pl.BlockSpec((p, dff), _used_tile_rows),
                      pl.BlockSpec(memory_space=pl.ANY),
                      pl.BlockSpec((1, 1, tn), lambda j, i, be, *_: (be[i], 0, j))],
            out_specs=pl.BlockSpec((p, tn), lambda j, i, *_: (i, j)),
            scratch_shapes=[pltpu.VMEM((2, 1, dff, tn), F32), pltpu.VMEM((dff, tn), BF16),
                            pltpu.SemaphoreType.DMA((2, 1))],
        ),
        compiler_params=pltpu.CompilerParams(
            dimension_semantics=("arbitrary", "arbitrary"), vmem_limit_bytes=_vmem_limit(est)),
        name="expert_down",
    )(*sched, hmid, wd, bd.reshape(n_e, 1, d))


def _combine_kernel(dest_ref, h_ref, gate_ref, gain_ref, ys_ref, o_ref, buf_ref, sem, *, tm, t):
    i = pl.program_id(0)
    slot = i % 2

    def row_copy(s, r, k, d):
        return pltpu.make_async_copy(ys_ref.at[pl.ds(d, 1)], buf_ref.at[s, k, pl.ds(r, 1)], sem.at[s])

    def gather(tile, s):
        def issue(r, c):
            for k in range(TOP_K):
                row_copy(s, r, k, dest_ref[(tile * tm + r) * TOP_K + k]).start()
            return c

        lax.fori_loop(0, tm, issue, 0)

    @pl.when(i == 0)
    def _():
        gather(0, 0)

    @pl.when(i + 1 < pl.num_programs(0))
    def _():
        gather(i + 1, 1 - slot)

    def drain(r, c):
        for k in range(TOP_K):
            row_copy(slot, r, k, 0).wait()
        return c

    lax.fori_loop(0, tm, drain, 0)

    acc = h_ref[...]
    gates = gate_ref[...]
    for k in range(TOP_K):
        acc = acc + gates[:, k:k + 1] * buf_ref[slot, k]
    o_ref[...] = _rms(acc, gain_ref[...])


def _combine(dest_flat, h, gates, gain, ys, tm):
    t, d = h.shape
    tm = min(tm, t)
    assert t % tm == 0
    est = 2 * TOP_K * tm * d * 4 + 4 * tm * d * 4 + 2 * tm * d * 4
    return pl.pallas_call(
        functools.partial(_combine_kernel, tm=tm, t=t),
        out_shape=jax.ShapeDtypeStruct((t, d), F32),
        grid_spec=pltpu.PrefetchScalarGridSpec(
            num_scalar_prefetch=1,
            grid=(t // tm,),
            in_specs=[pl.BlockSpec((tm, d), lambda i, dest: (i, 0)),
                      pl.BlockSpec((tm, TOP_K), lambda i, dest: (i, 0)),
                      pl.BlockSpec((1, d), lambda i, dest: (0, 0)),
                      pl.BlockSpec(memory_space=pl.ANY)],
            out_specs=pl.BlockSpec((tm, d), lambda i, dest: (i, 0)),
            scratch_shapes=[pltpu.VMEM((2, TOP_K, tm, d), F32), pltpu.SemaphoreType.DMA((2,))],
        ),
        compiler_params=pltpu.CompilerParams(
            dimension_semantics=("arbitrary",), vmem_limit_bytes=_vmem_limit(est)),
        name="combine",
    )(dest_flat, h, gates, gain, ys)


def kernel(x_prompt, x_sample, cache_k, cache_v, state_pool, norm_mix, w_in, attn_sinks, w_pool_mix,
           pool_scale, w_attn_out, w_pool_out, w_out, norm_ffn, w_router, b_router, w_gate, b_gate,
           w_up, b_up, w_down, b_down, norm_final):
    assert norm_mix.shape[0] == 1, "single-layer step"
    bp, sp, d = x_prompt.shape
    bs, ns, _ = x_sample.shape
    n_kv, hd = cache_k.shape[-2:]
    n_q = attn_sinks.shape[1]
    qw, kvw = n_q * hd, n_kv * hd
    pw = state_pool.shape[-1]
    n_e = w_router.shape[-1]
    tp_, ts_ = bp * sp, bs * ns
    t_all = tp_ + ts_

    gain_mix = norm_mix[0].reshape(1, d)
    w_in_bf = w_in[0].astype(BF16)
    w_q = w_in_bf[:, :qw]
    w_kvu = w_in_bf[:, qw:qw + 2 * kvw + pw]
    w_g = w_in_bf[:, qw + 2 * kvw + pw:]
    wmix = w_pool_mix[0].astype(BF16)
    pscale = pool_scale[0].reshape(1, pw)
    wa = w_attn_out[0].astype(BF16)
    wp = w_pool_out[0].astype(BF16)
    wo = w_out[0].astype(BF16)
    sinks = attn_sinks[0].astype(F32)

    def project(x2d):
        xn = _rms_cast(x2d, gain_mix, 512)
        q = _proj(xn, w_q, BF16, "scale", 1024, 1024, "proj_q")
        kvu = _proj(xn, w_kvu, F32, "none", 1024, 768, "proj_kvu")
        sg = _proj(xn, w_g, BF16, "sigmoid", 1024, 1024, "proj_gates")
        return q, kvu, sg

    xp = x_prompt.reshape(tp_, d)
    q_p, kvu_p, sg_p = project(xp)
    attn_p = _attn_prompt(q_p, kvu_p, sinks, sp, n_kv, hd)
    pool_p = _pool_prompt(kvu_p, 2 * kvw, pw, wmix, pscale, sp, 512)

    xs_ = x_sample.reshape(ts_, d)
    q_s, kvu_s, sg_s = project(xs_)
    k_s = kvu_s[:, :kvw].reshape(bs, ns, kvw)
    v_s = kvu_s[:, kvw:2 * kvw].reshape(bs, ns, kvw)
    u_s = kvu_s[:, 2 * kvw:].reshape(bs, ns, pw)
    ck = cache_k[0].reshape(bs, -1, kvw)
    cv = cache_v[0].reshape(bs, -1, kvw)
    o_s = _attn_sample(q_s.reshape(bs, ns, qw), k_s, v_s, ck, cv, sinks, n_kv, hd)
    attn_s = (o_s.reshape(bs, n_kv, n_q // n_kv, ns, hd).transpose(0, 3, 1, 2, 4)
              .reshape(ts_, qw).astype(BF16))
    ext = jnp.concatenate([state_pool[0], u_s], axis=1)
    pool_s = _pool_sample(ext.transpose(1, 0, 2), ns, wmix, pscale)
    pool_s = pool_s.transpose(1, 0, 2).reshape(ts_, pw)

    gain_ffn = norm_ffn[0].reshape(1, d)
    wr = w_router[0].T.astype(BF16)
    br = b_router[0].reshape(n_e, 1).astype(F32)
    tm = 256
    h_p, tok_p, idx_p, rank_p, gates_p, cnt_p = _merge_route(
        attn_p, pool_p, sg_p, xp, wa, wp, wo, gain_ffn, wr, br, jnp.zeros((n_e, 1), F32), tm)
    h_s, tok_s, idx_s, rank_s, gates_s, counts = _merge_route(
        attn_s, pool_s, sg_s, xs_, wa, wp, wo, gain_ffn, wr, br, cnt_p, tm)

    p = EXPERT_ROWS
    cnt = counts[:, 0].astype(I32)
    padded = (cnt + p - 1) // p * p
    pad_end = jnp.cumsum(padded)
    pad_start = pad_end - padded
    experts = jnp.arange(n_e, dtype=I32)[:, None, None]

    def slot_ids(idx, rank):
        start = jnp.sum(jnp.where(idx[None] == experts, pad_start[:, None, None], 0), axis=0)
        return (start + rank).T.reshape(-1)

    dest_p = slot_ids(idx_p, rank_p)
    dest_s = slot_ids(idx_s, rank_s)
    n_slots = -(-(t_all * TOP_K) // p) * p + n_e * p
    n_tiles = n_slots // p
    tile_start = jnp.arange(n_tiles, dtype=I32) * p
    blk_ok = (tile_start < pad_end[-1]).astype(I32)
    blk_e = jnp.sum((pad_end[None, :] <= tile_start[:, None]).astype(I32), axis=1)
    blk_e = jnp.minimum(blk_e, n_e - 1)
    blk_e = jnp.where(blk_ok == 1, blk_e, jnp.max(blk_e * blk_ok))
    ck = EXPERT_CHUNK
    chunk_start = jnp.arange(n_slots // ck, dtype=I32)[:, None] * ck
    used = jnp.any((pad_start[None, :] <= chunk_start) & (chunk_start < (pad_start + cnt)[None, :]),
                   axis=1).astype(I32)
    blk_nv = jnp.sum(used.reshape(n_tiles, p // ck), axis=1)
    pad_len = (cnt + ck - 1) // ck * ck - cnt

    meta = jnp.stack([jnp.sum(pad_len), used.shape[0] - jnp.sum(used)]).astype(I32)
    xs_sorted = _dispatch(dest_p, dest_s, pad_start + cnt, pad_len, used, meta, tok_p, tok_s, n_slots, tm)
    new_grp = jnp.concatenate([jnp.ones((1,), I32), (blk_e[1:] != blk_e[:-1]).astype(I32)])
    blk_grp = jnp.cumsum(new_grp) - 1
    n_groups = blk_grp[-1] + 1
    eids = jnp.arange(n_e, dtype=I32)
    order = jnp.cumsum((cnt > 0).astype(I32)) - 1
    grp_e = jnp.sum(jnp.where((cnt > 0)[None, :] & (order[None, :] == eids[:, None]), eids[None, :], 0), axis=1)
    blk_nxt = jnp.sum(jnp.where(eids[None, :] == (blk_grp + 1)[:, None], grp_e[None, :], 0), axis=1)
    blk_nxt = jnp.where(blk_grp + 1 < n_groups, blk_nxt, -1)
    sched = (blk_e, blk_nv, blk_grp, blk_nxt, jnp.stack([n_groups, grp_e[0]]))
    hmid = _expert_up(sched, xs_sorted, w_gate[0], w_up[0], b_gate[0], b_up[0], 1024)
    ys = _expert_down(sched, hmid, w_down[0], b_down[0], 1024)

    gain_fin = norm_final.reshape(1, d)
    y_p = _combine(dest_p, h_p, gates_p.T, gain_fin, ys, tm)
    y_s = _combine(dest_s, h_s, gates_s.T, gain_fin, ys, tm)

    keep = min(WINDOW, sp)
    tail_p = kvu_p.reshape(bp, sp, 2 * kvw + pw)[:, sp - keep:]
    n_rows = cache_k.shape[2]
    new_k_s = jnp.concatenate([cache_k[0], k_s.reshape(bs, ns, n_kv, hd)], axis=1)[:, -n_rows:]
    new_v_s = jnp.concatenate([cache_v[0], v_s.reshape(bs, ns, n_kv, hd)], axis=1)[:, -n_rows:]
    return (y_p.reshape(bp, sp, d), y_s.reshape(bs, ns, d),
            tail_p[:, :, :kvw].reshape(1, bp, keep, n_kv, hd),
            tail_p[:, :, kvw:2 * kvw].reshape(1, bp, keep, n_kv, hd),
            tail_p[None, :, keep - POOL_STATE:, 2 * kvw:],
            new_k_s[None], new_v_s[None], ext[None, :, -POOL_STATE:])
```
